```python
import math
import jax, jax.numpy as jnp
from jax import lax
import numpy as np

D_MODEL = 1024
BATCH = 8
SEQ = 4096
DEPTH = 4

GRID_W = 64
CTX_LEN = 256
N_GROUPS = 4
GROUP_W = D_MODEL // N_GROUPS
HEAD_DIM = 64
N_HEADS = GROUP_W // HEAD_DIM
NA_WIN_ROWS = 8
NA_WIN_COLS = 16
SGU_CHUNK = 128
DIFF_QK_DIM = HEAD_DIM // 2
ROPE_BASE = 10000.0
D_FF = 4 * D_MODEL
Q_BLOCK = 128
EPS = 1e-6
COL_FOURIER = 0
COL_NA = GROUP_W
COL_SGU = 4 * GROUP_W
COL_DIFF = 6 * GROUP_W
D_IN = 9 * GROUP_W

kernel_name = "hymba_style_fourier_na_sgu_diffattn_dit"


def rms_norm(x, g):
    x32 = x.astype(jnp.float32)
    y = x32 * lax.rsqrt(jnp.mean(x32 * x32, axis=-1, keepdims=True) + EPS)
    return (y * g.astype(jnp.float32)).astype(x.dtype)


def layer_norm(x, g, b):
    x32 = x.astype(jnp.float32)
    mu = jnp.mean(x32, axis=-1, keepdims=True)
    var = jnp.mean(jnp.square(x32 - mu), axis=-1, keepdims=True)
    y = (x32 - mu) * lax.rsqrt(var + EPS)
    return (y * g.astype(jnp.float32) + b.astype(jnp.float32)).astype(x.dtype)


def ada_params(cond, w, b):
    m = jax.nn.silu(cond) @ w + b
    return jnp.split(m, 6, axis=-1)


def modulate(h, shift, scale):
    return h * (1 + scale) + shift


def heads(t):
    return t.reshape(t.shape[0], t.shape[1], N_HEADS, HEAD_DIM)


def fourier_mix(a):
    bsz, n, _ = a.shape
    a4 = a.astype(jnp.float32).reshape(bsz, n, N_HEADS, GROUP_W // N_HEADS)
    f = jnp.fft.fft2(a4, axes=(1, 3), norm="ortho").real
    return f.reshape(bsz, n, GROUP_W).astype(a.dtype)


def dense_attention(q, k, v):
    s = jnp.einsum('bqhd,bkhd->bhqk', q, k).astype(jnp.float32) * (q.shape[-1] ** -0.5)
    p = jax.nn.softmax(s, axis=-1).astype(v.dtype)
    return jnp.einsum('bhqk,bkhd->bqhd', p, v)


def neighbourhood_attention(q, k, v, k_ctx, v_ctx, rpb, rows):
    bsz, n, h, dh = q.shape
    wr = min(NA_WIN_ROWS, rows)
    qg = q.reshape(bsz, rows, GRID_W, h, dh)
    kg = k.reshape(bsz, rows, GRID_W, h, dh)
    vg = v.reshape(bsz, rows, GRID_W, h, dh)
    r = jnp.arange(rows)
    row_start = jnp.clip(r - wr // 2, 0, rows - wr)
    key_rows = row_start[:, None] + jnp.arange(wr)[None, :]
    k_band = kg[:, key_rows]
    v_band = vg[:, key_rows]
    cidx = jnp.arange(GRID_W)
    col_start = jnp.clip(cidx - NA_WIN_COLS // 2, 0, GRID_W - NA_WIN_COLS)
    in_win = (cidx[None, :] >= col_start[:, None]) & (cidx[None, :] < col_start[:, None] + NA_WIN_COLS)
    dr = key_rows - r[:, None] + NA_WIN_ROWS - 1
    dc = jnp.clip(cidx[None, :] - cidx[:, None], 1 - NA_WIN_COLS, NA_WIN_COLS - 1) + NA_WIN_COLS - 1
    bias = rpb[:, dr[:, None, :, None], dc[None, :, None, :]]
    scale = dh ** -0.5
    s_loc = jnp.einsum('brqhd,brikhd->bhrqik', qg, k_band).astype(jnp.float32) * scale + bias.astype(jnp.float32)
    s_loc = jnp.where(in_win[:, None, :], s_loc, -jnp.inf)
    s_loc = s_loc.reshape(bsz, h, rows, GRID_W, wr * GRID_W)
    s_ctx = jnp.einsum('brqhd,blhd->bhrql', qg, k_ctx).astype(jnp.float32) * scale
    p = jax.nn.softmax(jnp.concatenate([s_loc, s_ctx], axis=-1), axis=-1).astype(v.dtype)
    p_loc = p[..., :wr * GRID_W].reshape(bsz, h, rows, GRID_W, wr, GRID_W)
    p_ctx = p[..., wr * GRID_W:]
    o = jnp.einsum('bhrqik,brikhd->brqhd', p_loc, v_band) + jnp.einsum('bhrql,blhd->brqhd', p_ctx, v_ctx)
    return o.reshape(bsz, n, h * dh)


def spatial_gating(uv, ln_g, ln_b, w_s, b_s):
    bsz, n, _ = uv.shape
    u, v = jnp.split(jax.nn.gelu(uv), 2, axis=-1)
    v = layer_norm(v, ln_g, ln_b)
    vc = v.reshape(bsz, n // SGU_CHUNK, SGU_CHUNK, N_HEADS, GROUP_W // N_HEADS)
    s = jnp.einsum('gpq,bnqgc->bnpgc', w_s, vc) + b_s.T[None, None, :, :, None]
    return u * s.reshape(bsz, n, GROUP_W)


def axial_rope(n):
    t = jnp.arange(n)
    rows = (t // GRID_W).astype(jnp.float32)
    cols = (t % GRID_W).astype(jnp.float32)
    n_freq = DIFF_QK_DIM // 4
    inv = ROPE_BASE ** (-jnp.arange(n_freq, dtype=jnp.float32) / n_freq)
    ang = jnp.concatenate([rows[:, None] * inv, cols[:, None] * inv], axis=-1)
    return jnp.cos(ang), jnp.sin(ang)


def apply_rope(x, cos, sin):
    nf = DIFF_QK_DIM // 4
    xs = x.reshape(x.shape[:-1] + (2, 2, nf))
    x1, x2 = xs[..., 0, :], xs[..., 1, :]
    c = cos.reshape(cos.shape[0], 1, 2, nf).astype(x.dtype)
    s = sin.reshape(sin.shape[0], 1, 2, nf).astype(x.dtype)
    out = jnp.stack([x1 * c - x2 * s, x1 * s + x2 * c], axis=-2)
    return out.reshape(x.shape)


def diff_attend(q1, q2, k1, k2, v, lam):
    scale = DIFF_QK_DIM ** -0.5
    s1 = jnp.einsum('bqhd,bkhd->bhqk', q1, k1).astype(jnp.float32) * scale
    s2 = jnp.einsum('bqhd,bkhd->bhqk', q2, k2).astype(jnp.float32) * scale
    p = jax.nn.softmax(s1, axis=-1) - lam * jax.nn.softmax(s2, axis=-1)
    return jnp.einsum('bhqk,bkhd->bqhd', p.astype(v.dtype), v)


def diff_latent(q, k, v, k_ctx, v_ctx, lam, cos, sin):
    bsz, n, h, _ = q.shape
    q1 = apply_rope(q[..., :DIFF_QK_DIM], cos, sin)
    q2 = apply_rope(q[..., DIFF_QK_DIM:], cos, sin)
    k1 = jnp.concatenate([apply_rope(k[..., :DIFF_QK_DIM], cos, sin), k_ctx[..., :DIFF_QK_DIM]], axis=1)
    k2 = jnp.concatenate([apply_rope(k[..., DIFF_QK_DIM:], cos, sin), k_ctx[..., DIFF_QK_DIM:]], axis=1)
    v_all = jnp.concatenate([v, v_ctx], axis=1)
    nb = n // Q_BLOCK

    def to_blocks(t):
        return t.reshape(bsz, nb, Q_BLOCK, h, DIFF_QK_DIM).swapaxes(0, 1)

    o = lax.map(lambda qb: diff_attend(qb[0], qb[1], k1, k2, v_all, lam), (to_blocks(q1), to_blocks(q2)))
    return o.swapaxes(0, 1).reshape(bsz, n, h, HEAD_DIM)


def diff_post(o, g, lam_init):
    y = rms_norm(o, g) * (1 - lam_init)
    return y.reshape(o.shape[0], o.shape[1], GROUP_W)


def sq_relu_mlp(h, w1, w2):
    return jnp.square(jax.nn.relu(h @ w1)) @ w2


def setup_inputs(seed: int = 0) -> dict:
    key = jax.random.key(seed)
    ks = jax.random.split(key, 24)
    f32 = jnp.float32

    def nrm(k, shape, s):
        return jax.random.normal(k, shape, f32) * s

    return {
        "x": nrm(ks[0], (BATCH, SEQ, D_MODEL), 1.0),
        "c": nrm(ks[1], (BATCH, D_MODEL), 1.0),
        "ctx": nrm(ks[2], (BATCH, CTX_LEN, D_MODEL), 1.0),
        "c_ctx": nrm(ks[3], (D_MODEL,), 1.0),
        "ada_w": nrm(ks[4], (DEPTH, D_MODEL, 6 * D_MODEL), 0.5 * D_MODEL ** -0.5),
        "ada_b": nrm(ks[5], (DEPTH, 6 * D_MODEL), 0.01),
        "norm1_g": 1.0 + nrm(ks[6], (DEPTH, D_MODEL), 0.02),
        "norm2_g": 1.0 + nrm(ks[7], (DEPTH, D_MODEL), 0.02),
        "w_in": nrm(ks[8], (DEPTH, D_MODEL, D_IN), D_MODEL ** -0.5),
        "w_out": nrm(ks[9], (DEPTH, D_MODEL, D_MODEL), D_MODEL ** -0.5),
        "na_rpb": nrm(ks[10], (DEPTH, N_HEADS, 2 * NA_WIN_ROWS - 1, 2 * NA_WIN_COLS - 1), 0.1),
        "sgu_ln_g": 1.0 + nrm(ks[11], (DEPTH, GROUP_W), 0.02),
        "sgu_ln_b": nrm(ks[12], (DEPTH, GROUP_W), 0.02),
        "sgu_w": nrm(ks[13], (DEPTH, N_HEADS, SGU_CHUNK, SGU_CHUNK), SGU_CHUNK ** -0.5),
        "sgu_b": 1.0 + nrm(ks[14], (DEPTH, N_HEADS, SGU_CHUNK), 0.02),
        "diff_lq1": nrm(ks[15], (DEPTH, DIFF_QK_DIM), 0.1),
        "diff_lk1": nrm(ks[16], (DEPTH, DIFF_QK_DIM), 0.1),
        "diff_lq2": nrm(ks[17], (DEPTH, DIFF_QK_DIM), 0.1),
        "diff_lk2": nrm(ks[18], (DEPTH, DIFF_QK_DIM), 0.1),
        "diff_subln_g": 1.0 + nrm(ks[19], (DEPTH, HEAD_DIM), 0.02),
        "w_ff1": nrm(ks[20], (DEPTH, D_MODEL, D_FF), D_MODEL ** -0.5),
        "w_ff2": nrm(ks[21], (DEPTH, D_FF, D_MODEL), D_FF ** -0.5),
        "final_g": 1.0 + nrm(ks[22], (D_MODEL,), 0.02),
    }


def reference(x, c, ctx, c_ctx, ada_w, ada_b, norm1_g, norm2_g, w_in, w_out, na_rpb, sgu_ln_g, sgu_ln_b,
              sgu_w, sgu_b, diff_lq1, diff_lk1, diff_lq2, diff_lk2, diff_subln_g, w_ff1, w_ff2, final_g):
    bsz, n, _ = x.shape
    rows = n // GRID_W
    cos, sin = axial_rope(n)
    cx = ctx
    gw = GROUP_W
    for l in range(DEPTH):
        last = l == DEPTH - 1
        sh1, sc1, g1, sh2, sc2, g2 = ada_params(c[:, None, :], ada_w[l], ada_b[l])
        csh1, csc1, cg1, csh2, csc2, cg2 = ada_params(c_ctx[None, None, :], ada_w[l], ada_b[l])
        h = modulate(rms_norm(x, norm1_g[l]), sh1, sc1)
        hc = modulate(rms_norm(cx, norm1_g[l]), csh1, csc1)
        w = w_in[l]
        p = h @ w
        if last:
            na_kv_c = hc @ w[:, COL_NA + gw:COL_NA + 3 * gw]
            df_kv_c = hc @ w[:, COL_DIFF + gw:COL_DIFF + 3 * gw]
        else:
            pc = hc @ w
            na_kv_c = pc[..., COL_NA + gw:COL_NA + 3 * gw]
            df_kv_c = pc[..., COL_DIFF + gw:COL_DIFF + 3 * gw]
        na_kc, na_vc = heads(na_kv_c[..., :gw]), heads(na_kv_c[..., gw:])
        df_kc, df_vc = heads(df_kv_c[..., :gw]), heads(df_kv_c[..., gw:])
        lam_init = 0.8 - 0.6 * math.exp(-0.3 * l)
        lam = (jnp.exp(jnp.sum(diff_lq1[l].astype(jnp.float32) * diff_lk1[l].astype(jnp.float32)))
               - jnp.exp(jnp.sum(diff_lq2[l].astype(jnp.float32) * diff_lk2[l].astype(jnp.float32))) + lam_init)

        y_a = fourier_mix(p[..., COL_FOURIER:COL_FOURIER + gw])
        y_b = neighbourhood_attention(heads(p[..., COL_NA:COL_NA + gw]), heads(p[..., COL_NA + gw:COL_NA + 2 * gw]),
                                      heads(p[..., COL_NA + 2 * gw:COL_NA + 3 * gw]), na_kc, na_vc, na_rpb[l], rows)
        y_c = spatial_gating(p[..., COL_SGU:COL_SGU + 2 * gw], sgu_ln_g[l], sgu_ln_b[l], sgu_w[l], sgu_b[l])
        o_d = diff_latent(heads(p[..., COL_DIFF:COL_DIFF + gw]), heads(p[..., COL_DIFF + gw:COL_DIFF + 2 * gw]),
                          heads(p[..., COL_DIFF + 2 * gw:COL_DIFF + 3 * gw]), df_kc, df_vc, lam, cos, sin)
        y_d = diff_post(o_d, diff_subln_g[l], lam_init)
        y = jnp.concatenate([y_a, y_b, y_c, y_d], axis=-1) @ w_out[l]

        if not last:
            yc_a = fourier_mix(pc[..., COL_FOURIER:COL_FOURIER + gw])
            yc_b = dense_attention(heads(pc[..., COL_NA:COL_NA + gw]), na_kc, na_vc).reshape(bsz, -1, gw)
            yc_c = spatial_gating(pc[..., COL_SGU:COL_SGU + 2 * gw], sgu_ln_g[l], sgu_ln_b[l], sgu_w[l], sgu_b[l])
            qc = heads(pc[..., COL_DIFF:COL_DIFF + gw])
            oc_d = diff_attend(qc[..., :DIFF_QK_DIM], qc[..., DIFF_QK_DIM:], df_kc[..., :DIFF_QK_DIM],
                               df_kc[..., DIFF_QK_DIM:], df_vc, lam)
            yc_d = diff_post(oc_d, diff_subln_g[l], lam_init)
            yc = jnp.concatenate([yc_a, yc_b, yc_c, yc_d], axis=-1) @ w_out[l]
            cx = cx + cg1 * yc
            hc2 = modulate(rms_norm(cx, norm2_g[l]), csh2, csc2)
            cx = cx + cg2 * sq_relu_mlp(hc2, w_ff1[l], w_ff2[l])

        x = x + g1 * y
        h2 = modulate(rms_norm(x, norm2_g[l]), sh2, sc2)
        x = x + g2 * sq_relu_mlp(h2, w_ff1[l], w_ff2[l])
    return rms_norm(x, final_g)
```

```python
import math
from functools import partial

import numpy as np
import jax
import jax.numpy as jnp
from jax import lax
from jax.experimental import pallas as pl
from jax.experimental.pallas import tpu as pltpu

D_MODEL = 1024
GRID_W = 64
N_HEADS = 4
GROUP_W = 256
HEAD_DIM = 64
NA_WIN_ROWS = 8
NA_WIN_COLS = 16
SGU_CHUNK = 128
DIFF_QK_DIM = 32
ROPE_BASE = 10000.0
D_FF = 4 * D_MODEL
D_IN = 9 * GROUP_W
EPS = 1e-6

LOG2E = 1.4426950408889634
NA_Q_SCALE = HEAD_DIM ** -0.5 * LOG2E
DIFF_Q_SCALE = DIFF_QK_DIM ** -0.5 * LOG2E
MASKED = -1e30
COND_ROWS = 16
KEY_CHUNK = 256
NA_Q_ROWS = 8
NA_BAND_ROWS = 16
V7X_VMEM_LIMIT = 56 * 1024 * 1024

f32 = jnp.float32
bf16 = jnp.bfloat16


def _dot(a, b):
    return jnp.dot(a, b, preferred_element_type=f32)


def _dot_nt(a, b):
    return lax.dot_general(a, b, (((1,), (1,)), ((), ())), preferred_element_type=f32)


def _ada_kernel(cond_ref, w_ref, b_ref, o_ref):
    c = cond_ref[...]
    s = (c * jax.nn.sigmoid(c)).astype(bf16)
    o_ref[0] = _dot(s, w_ref[0].astype(bf16)) + b_ref[0]


def _ada_mods(cond, ada_w, ada_b):
    depth = ada_w.shape[0]
    tn = 1536
    return pl.pallas_call(
        _ada_kernel,
        grid=(depth, 6 * D_MODEL // tn),
        in_specs=[
            pl.BlockSpec((COND_ROWS, D_MODEL), lambda l, j: (0, 0)),
            pl.BlockSpec((1, D_MODEL, tn), lambda l, j: (l, 0, j)),
            pl.BlockSpec((1, 1, tn), lambda l, j: (l, 0, j)),
        ],
        out_specs=pl.BlockSpec((1, COND_ROWS, tn), lambda l, j: (l, 0, j)),
        out_shape=jax.ShapeDtypeStruct((depth, COND_ROWS, 6 * D_MODEL), f32),
        compiler_params=pltpu.CompilerParams(vmem_limit_bytes=V7X_VMEM_LIMIT),
        name="ada",
    )(cond, ada_w, ada_b.reshape(depth, 1, 6 * D_MODEL))


def _mod_spec(layer, which, ctx):
    if ctx:
        return pl.BlockSpec((1, 1, D_MODEL), lambda b, i: ((layer * COND_ROWS + COND_ROWS // 2) * 6 + which, 0, 0))
    return pl.BlockSpec((1, 1, D_MODEL), lambda b, i: ((layer * COND_ROWS + b) * 6 + which, 0, 0))


def _rope_partner(x):
    lane = lax.broadcasted_iota(jnp.int32, x.shape, 1)
    first_half = (lane % 16) < 8
    n = x.shape[1]
    return jnp.where(first_half, pltpu.roll(x, n - 8, 1), pltpu.roll(x, 8, 1))


def _proj_in_kernel(x_ref, sh_ref, sc_ref, g_ref, w_ref, bd_ref, cq_ref, sq_ref, ck_ref, sk_ref, lng_ref, lnb_ref,
                    acs_ref, naq_ref, nak_ref, nav_ref, sgu_ref, sgv_ref, dfq_ref, dfk_ref, dfv_ref):
    x = x_ref[0]
    r = lax.rsqrt(jnp.mean(x * x, axis=-1, keepdims=True) + EPS)
    h = (x * r) * g_ref[...]
    h = h * (1.0 + sc_ref[0]) + sh_ref[0]
    hb = h.astype(bf16)
    gw = GROUP_W

    def mm(col, width=gw):
        return _dot(hb, w_ref[:, col:col + width])

    acs_ref[0] = _dot(mm(0).astype(bf16), bd_ref[...]).astype(bf16)
    naq_ref[0] = (mm(gw) * NA_Q_SCALE).astype(bf16)
    nak_ref[0] = mm(2 * gw).astype(bf16)
    nav_ref[0] = mm(3 * gw).astype(bf16)
    uv = jax.nn.gelu(mm(4 * gw, 2 * gw), approximate=True)
    sgu_ref[0] = uv[:, :gw].astype(bf16)
    v = uv[:, gw:]
    mu = jnp.mean(v, axis=-1, keepdims=True)
    vc = v - mu
    var = jnp.mean(vc * vc, axis=-1, keepdims=True)
    sgv_ref[0] = (vc * lax.rsqrt(var + EPS) * lng_ref[...] + lnb_ref[...]).astype(bf16)
    q = mm(6 * gw)
    dfq_ref[0] = (q * cq_ref[...] + _rope_partner(q) * sq_ref[...]).astype(bf16)
    k = mm(7 * gw)
    dfk_ref[0] = (k * ck_ref[...] + _rope_partner(k) * sk_ref[...]).astype(bf16)
    dfv_ref[0] = mm(8 * gw).astype(bf16)


def _proj_in(x, mods, layer, ctx, g, w, bd, rope, lng, lnb):
    bsz, s, _ = x.shape
    tm = min(s, 512)
    row = lambda b, i: (b, i, 0)
    const = lambda b, i: (0, 0)
    pos = lambda b, i: (i, 0)
    out = lambda width: pl.BlockSpec((1, tm, width), row)
    shape = lambda width: jax.ShapeDtypeStruct((bsz, s, width), bf16)
    widths = (2 * GROUP_W,) + (GROUP_W,) * 8
    return pl.pallas_call(
        _proj_in_kernel,
        grid=(bsz, s // tm),
        in_specs=[
            pl.BlockSpec((1, tm, D_MODEL), row),
            _mod_spec(layer, 0, ctx), _mod_spec(layer, 1, ctx),
            pl.BlockSpec((1, D_MODEL), const),
            pl.BlockSpec((D_MODEL, D_IN), const),
            pl.BlockSpec((GROUP_W, 2 * GROUP_W), const),
            pl.BlockSpec((tm, GROUP_W), pos), pl.BlockSpec((tm, GROUP_W), pos),
            pl.BlockSpec((tm, GROUP_W), pos), pl.BlockSpec((tm, GROUP_W), pos),
            pl.BlockSpec((1, GROUP_W), const), pl.BlockSpec((1, GROUP_W), const),
        ],
        out_specs=[out(wd) for wd in widths],
        out_shape=[shape(wd) for wd in widths],
        compiler_params=pltpu.CompilerParams(vmem_limit_bytes=V7X_VMEM_LIMIT),
        name="proj_in_ctx" if ctx else "proj_in",
    )(x, mods, mods, g, w, bd, *rope, lng, lnb)


def _fourier_kernel(c_ref, s_ref, a_ref, o_ref, acc_ref):
    k = pl.program_id(1)
    gw = GROUP_W

    @pl.when(k == 0)
    def _():
        acc_ref[...] = jnp.zeros_like(acc_ref)

    for b in range(a_ref.shape[0]):
        acc_ref[b] += _dot(c_ref[...], a_ref[b, :, :gw]) + _dot(s_ref[...], a_ref[b, :, gw:])

    @pl.when(k == pl.num_programs(1) - 1)
    def _():
        o_ref[...] = acc_ref[...].astype(o_ref.dtype)


def _fourier(cn, sn, acs):
    bsz, s, _ = acs.shape
    tm = min(s, 1024)
    tk = min(s, 512)
    return pl.pallas_call(
        _fourier_kernel,
        grid=(s // tm, s // tk),
        in_specs=[
            pl.BlockSpec((tm, tk), lambda i, k: (i, k)),
            pl.BlockSpec((tm, tk), lambda i, k: (i, k)),
            pl.BlockSpec((bsz, tk, 2 * GROUP_W), lambda i, k: (0, k, 0)),
        ],
        out_specs=pl.BlockSpec((bsz, tm, GROUP_W), lambda i, k: (0, i, 0)),
        out_shape=jax.ShapeDtypeStruct((bsz, s, GROUP_W), bf16),
        scratch_shapes=[pltpu.VMEM((bsz, tm, GROUP_W), f32)],
        compiler_params=pltpu.CompilerParams(vmem_limit_bytes=V7X_VMEM_LIMIT),
        name="fourier",
    )(cn, sn, acs)


def _na_band_start(i):
    rows = GRID_W
    return jnp.clip(i * NA_Q_ROWS - NA_WIN_ROWS // 2, 0, rows - NA_BAND_ROWS)


def _na_kernel(q_ref, k_ref, v_ref, kc_ref, vc_ref, bias_ref, o_ref):
    i = pl.program_id(0)
    start = pl.multiple_of(_na_band_start(i) * GRID_W, 256)
    kb = k_ref[0, pl.ds(start, NA_BAND_ROWS * GRID_W), :]
    vb = v_ref[0, pl.ds(start, NA_BAND_ROWS * GRID_W), :]
    kc = kc_ref[0]
    vc = vc_ref[0]
    q = q_ref[0]
    lane = lax.broadcasted_iota(jnp.int32, q.shape, 1)
    out = jnp.zeros(q.shape, f32)
    for h in range(N_HEADS):
        in_head = (lane >= h * HEAD_DIM) & (lane < (h + 1) * HEAD_DIM)
        qh = jnp.where(in_head, q, jnp.zeros_like(q))
        s_loc = _dot_nt(qh, kb) + bias_ref[0, h]
        s_ctx = _dot_nt(qh, kc)
        m = jnp.maximum(jnp.max(s_loc, axis=-1, keepdims=True), jnp.max(s_ctx, axis=-1, keepdims=True))
        p_loc = jnp.exp2(s_loc - m)
        p_ctx = jnp.exp2(s_ctx - m)
        l = jnp.sum(p_loc, axis=-1, keepdims=True) + jnp.sum(p_ctx, axis=-1, keepdims=True)
        o = _dot(p_loc.astype(bf16), vb) + _dot(p_ctx.astype(bf16), vc)
        out = jnp.where(in_head, o / l, out)
    o_ref[0] = out.astype(o_ref.dtype)


def _na(q, k, v, kc, vc, bias):
    bsz, s, _ = q.shape
    tq = NA_Q_ROWS * GRID_W
    nblk = s // tq
    band = NA_BAND_ROWS * GRID_W

    def bias_map(i, b):
        return (jnp.where(i == 0, 0, jnp.where(i == nblk - 1, 2, 1)), 0, 0, 0)

    full = lambda n: pl.BlockSpec((1, n, GROUP_W), lambda i, b: (b, 0, 0))
    return pl.pallas_call(
        _na_kernel,
        grid=(nblk, bsz),
        in_specs=[
            pl.BlockSpec((1, tq, GROUP_W), lambda i, b: (b, i, 0)),
            full(s), full(s), full(kc.shape[1]), full(kc.shape[1]),
            pl.BlockSpec((1, N_HEADS, tq, band), bias_map),
        ],
        out_specs=pl.BlockSpec((1, tq, GROUP_W), lambda i, b: (b, i, 0)),
        out_shape=jax.ShapeDtypeStruct((bsz, s, GROUP_W), bf16),
        compiler_params=pltpu.CompilerParams(vmem_limit_bytes=V7X_VMEM_LIMIT),
        name="na",
    )(q, k, v, kc, vc, bias)


def _na_bias(rpb, rows):
    cidx = np.arange(GRID_W)
    col_start = np.clip(cidx - NA_WIN_COLS // 2, 0, GRID_W - NA_WIN_COLS)
    in_win = (cidx[None, :] >= col_start[:, None]) & (cidx[None, :] < col_start[:, None] + NA_WIN_COLS)
    dc = np.clip(cidx[None, :] - cidx[:, None], 1 - NA_WIN_COLS, NA_WIN_COLS - 1) + NA_WIN_COLS - 1
    t = jnp.where(in_win[None, None], rpb[:, :, dc] * LOG2E, MASKED)
    t = jnp.concatenate([t, jnp.full_like(t[:, :1], MASKED)], axis=1)
    nblk = rows // NA_Q_ROWS
    dr_idx = np.full((3, NA_Q_ROWS, NA_BAND_ROWS), 2 * NA_WIN_ROWS - 1, np.int32)
    for cat, blk in enumerate((0, 1, nblk - 1)):
        band0 = int(np.clip(blk * NA_Q_ROWS - NA_WIN_ROWS // 2, 0, rows - NA_BAND_ROWS))
        for qi in range(NA_Q_ROWS):
            r = blk * NA_Q_ROWS + qi
            r0 = int(np.clip(r - NA_WIN_ROWS // 2, 0, rows - NA_WIN_ROWS))
            for kj in range(NA_BAND_ROWS):
                kr = band0 + kj
                if r0 <= kr < r0 + NA_WIN_ROWS:
                    dr_idx[cat, qi, kj] = kr - r + NA_WIN_ROWS - 1
    big = t[:, dr_idx]
    big = big.transpose(1, 0, 2, 4, 3, 5)
    return big.reshape(3, N_HEADS, NA_Q_ROWS * GRID_W, NA_BAND_ROWS * GRID_W)


def _sgu_kernel(u_ref, v_ref, w_ref, b_ref, o_ref):
    tm = u_ref.shape[1]
    lane = lax.broadcasted_iota(jnp.int32, (SGU_CHUNK, GROUP_W), 1)
    for c in range(tm // SGU_CHUNK):
        rows = slice(c * SGU_CHUNK, (c + 1) * SGU_CHUNK)
        vchunk = v_ref[0, rows, :]
        s = b_ref[...]
        for g in range(N_HEADS):
            in_group = (lane >= g * HEAD_DIM) & (lane < (g + 1) * HEAD_DIM)
            s = jnp.where(in_group, s + _dot(w_ref[g], vchunk), s)
        o_ref[0, rows, :] = (u_ref[0, rows, :].astype(f32) * s).astype(o_ref.dtype)


def _sgu(u, v, w, b_full):
    bsz, s, _ = u.shape
    tm = min(s, 512)
    row = pl.BlockSpec((1, tm, GROUP_W), lambda b, i: (b, i, 0))
    return pl.pallas_call(
        _sgu_kernel,
        grid=(bsz, s // tm),
        in_specs=[row, row,
                  pl.BlockSpec((N_HEADS, SGU_CHUNK, SGU_CHUNK), lambda b, i: (0, 0, 0)),
                  pl.BlockSpec((SGU_CHUNK, GROUP_W), lambda b, i: (0, 0))],
        out_specs=row,
        out_shape=jax.ShapeDtypeStruct((bsz, s, GROUP_W), bf16),
        name="sgu",
    )(u, v, w, b_full)


def _attn_t_kernel(lam_ref, q_ref, k_ref, vt_ref, g_ref, o_ref, qm_ref, m_ref, acc_ref, *, n_maps, post_scale):
    n_streams = N_HEADS * n_maps
    width = GROUP_W // n_streams
    q = q_ref[0]
    lane = lax.broadcasted_iota(jnp.int32, q.shape, 1)
    for j in range(n_streams):
        qm_ref[j] = jnp.where((lane >= j * width) & (lane < (j + 1) * width), q, jnp.zeros_like(q))
    m_ref[...] = jnp.full(m_ref.shape, MASKED, f32)
    acc_ref[...] = jnp.zeros_like(acc_ref)

    def chunk(c, carry):
        kc = k_ref[0, c]
        for j in range(n_streams):
            s = _dot_nt(kc, qm_ref[j])
            m_old = m_ref[j]
            m_new = jnp.maximum(m_old, jnp.max(s, axis=0, keepdims=True))
            alpha = jnp.exp2(m_old - m_new)
            p = jnp.exp2(s - m_new).astype(bf16)
            acc_ref[j] = acc_ref[j] * alpha + _dot(vt_ref[0, c, j // n_maps], p)
            m_ref[j] = m_new
        return carry

    lax.fori_loop(0, k_ref.shape[1], chunk, 0)

    outs = []
    for h in range(N_HEADS):
        if n_maps == 1:
            a = acc_ref[h]
            o = a[:HEAD_DIM] / a[HEAD_DIM:HEAD_DIM + 1]
        else:
            a1 = acc_ref[2 * h]
            a2 = acc_ref[2 * h + 1]
            o = a1[:HEAD_DIM] / a1[HEAD_DIM:HEAD_DIM + 1] - lam_ref[0] * (a2[:HEAD_DIM] / a2[HEAD_DIM:HEAD_DIM + 1])
            r = lax.rsqrt(jnp.mean(o * o, axis=0, keepdims=True) + EPS)
            o = (o * r) * g_ref[...] * post_scale
        outs.append(o)
    o_ref[0] = jnp.concatenate(outs, axis=0).T.astype(o_ref.dtype)


def _attn_t(q, k_chunks, vt_chunks, lam, g_rows, n_maps, post_scale):
    bsz, s, _ = q.shape
    tq = min(s, 512)
    n_chunks = k_chunks.shape[1]
    n_streams = N_HEADS * n_maps
    return pl.pallas_call(
        partial(_attn_t_kernel, n_maps=n_maps, post_scale=post_scale),
        grid=(bsz, s // tq),
        in_specs=[
            pl.BlockSpec(memory_space=pltpu.SMEM),
            pl.BlockSpec((1, tq, GROUP_W), lambda b, i: (b, i, 0)),
            pl.BlockSpec((1, n_chunks, KEY_CHUNK, GROUP_W), lambda b, i: (b, 0, 0, 0)),
            pl.BlockSpec((1, n_chunks, N_HEADS, 2 * HEAD_DIM, KEY_CHUNK), lambda b, i: (b, 0, 0, 0, 0)),
            pl.BlockSpec((HEAD_DIM, tq), lambda b, i: (0, 0)),
        ],
        out_specs=pl.BlockSpec((1, tq, GROUP_W), lambda b, i: (b, i, 0)),
        out_shape=jax.ShapeDtypeStruct((bsz, s, GROUP_W), bf16),
        scratch_shapes=[
            pltpu.VMEM((n_streams, tq, GROUP_W), bf16),
            pltpu.VMEM((n_streams, 1, tq), f32),
            pltpu.VMEM((n_streams, 2 * HEAD_DIM, tq), f32),
        ],
        compiler_params=pltpu.CompilerParams(vmem_limit_bytes=V7X_VMEM_LIMIT),
        name="attn_t%d" % n_maps,
    )(lam, q, k_chunks, vt_chunks, g_rows)


def _key_chunks(k):
    bsz, n, w = k.shape
    return k.reshape(bsz, n // KEY_CHUNK, KEY_CHUNK, w)


def _value_chunks(v):
    bsz, n, _ = v.shape
    vt = v.reshape(bsz, n // KEY_CHUNK, KEY_CHUNK, N_HEADS, HEAD_DIM).transpose(0, 1, 3, 4, 2)
    return jnp.concatenate([vt, jnp.ones_like(vt)], axis=3)


def _post_kernel(x_ref, ya_ref, yb_ref, yc_ref, yd_ref, g1_ref, sh_ref, sc_ref, g2_ref, ng_ref, wo_ref, w1_ref,
                 w2_ref, fg_ref, o_ref, *, final):
    y = jnp.concatenate([ya_ref[0], yb_ref[0], yc_ref[0], yd_ref[0]], axis=-1)
    x = x_ref[0] + g1_ref[0] * _dot(y, wo_ref[...])
    r = lax.rsqrt(jnp.mean(x * x, axis=-1, keepdims=True) + EPS)
    h = (x * r) * ng_ref[...]
    hb = (h * (1.0 + sc_ref[0]) + sh_ref[0]).astype(bf16)
    acc = jnp.zeros(x.shape, f32)
    step = D_MODEL
    for j in range(D_FF // step):
        t = jnp.maximum(_dot(hb, w1_ref[:, j * step:(j + 1) * step]), 0.0)
        acc = acc + _dot((t * t).astype(bf16), w2_ref[j * step:(j + 1) * step, :])
    x = x + g2_ref[0] * acc
    if final:
        r = lax.rsqrt(jnp.mean(x * x, axis=-1, keepdims=True) + EPS)
        x = (x * r) * fg_ref[...]
    o_ref[0] = x


def _post(x, ys, mods, layer, ctx, ng, wo, w1, w2, fg, final):
    bsz, s, _ = x.shape
    tm = min(s, 512)
    row = lambda width: pl.BlockSpec((1, tm, width), lambda b, i: (b, i, 0))
    const = lambda shape: pl.BlockSpec(shape, lambda b, i: (0, 0), pipeline_mode=pl.Buffered(1))
    return pl.pallas_call(
        partial(_post_kernel, final=final),
        grid=(bsz, s // tm),
        in_specs=[
            row(D_MODEL), row(GROUP_W), row(GROUP_W), row(GROUP_W), row(GROUP_W),
            _mod_spec(layer, 2, ctx), _mod_spec(layer, 3, ctx), _mod_spec(layer, 4, ctx), _mod_spec(layer, 5, ctx),
            const((1, D_MODEL)), const((D_MODEL, D_MODEL)), const((D_MODEL, D_FF)), const((D_FF, D_MODEL)),
            const((1, D_MODEL)),
        ],
        out_specs=row(D_MODEL),
        out_shape=jax.ShapeDtypeStruct((bsz, s, D_MODEL), f32),
        compiler_params=pltpu.CompilerParams(vmem_limit_bytes=V7X_VMEM_LIMIT),
        name="post_ctx" if ctx else "post",
    )(x, *ys, mods, mods, mods, mods, ng, wo, w1, w2, fg)


def _channel_dft(n_pos):
    c = np.arange(HEAD_DIM)
    ang = 2.0 * np.pi * ((c[:, None] * c[None, :]) % HEAD_DIM) / HEAD_DIM
    scale = 1.0 / math.sqrt(n_pos * HEAD_DIM)
    bd = np.zeros((GROUP_W, 2 * GROUP_W), np.float32)
    for g in range(N_HEADS):
        sl = slice(g * HEAD_DIM, (g + 1) * HEAD_DIM)
        bd[sl, sl] = np.cos(ang) * scale
        bd[sl, GROUP_W + g * HEAD_DIM:GROUP_W + (g + 1) * HEAD_DIM] = -np.sin(ang) * scale
    return jnp.asarray(bd, bf16)


def _position_dft(n_pos):
    hi = max(n_pos // GRID_W, 1)
    lo = n_pos // hi
    np_ = jnp.arange(n_pos, dtype=jnp.int32)[:, None]
    a = 2.0 * np.pi * ((np_ * jnp.arange(hi, dtype=jnp.int32)[None, :] * lo) % n_pos).astype(f32) / n_pos
    b = 2.0 * np.pi * ((np_ * jnp.arange(lo, dtype=jnp.int32)[None, :]) % n_pos).astype(f32) / n_pos
    ca, sa, cb, sb = jnp.cos(a)[:, :, None], jnp.sin(a)[:, :, None], jnp.cos(b)[:, None, :], jnp.sin(b)[:, None, :]
    cn = (ca * cb - sa * sb).reshape(n_pos, n_pos)
    sn = (sa * cb + ca * sb).reshape(n_pos, n_pos)
    return cn.astype(bf16), sn.astype(bf16)


def _rope_tables(n):
    t = jnp.arange(n)
    rows = (t // GRID_W).astype(f32)
    cols = (t % GRID_W).astype(f32)
    n_freq = DIFF_QK_DIM // 4
    inv = ROPE_BASE ** (-jnp.arange(n_freq, dtype=f32) / n_freq)
    ang = jnp.concatenate([rows[:, None] * inv, cols[:, None] * inv], axis=-1)
    lane = np.arange(GROUP_W) % DIFF_QK_DIM
    src = (lane // 16) * n_freq + lane % n_freq
    sign = np.where((lane % 16) < n_freq, -1.0, 1.0).astype(np.float32)
    return jnp.cos(ang)[:, src], jnp.sin(ang)[:, src] * sign


def kernel(x, c, ctx, c_ctx, ada_w, ada_b, norm1_g, norm2_g, w_in, w_out, na_rpb, sgu_ln_g, sgu_ln_b, sgu_w, sgu_b,
           diff_lq1, diff_lk1, diff_lq2, diff_lk2, diff_subln_g, w_ff1, w_ff2, final_g):
    bsz, n, _ = x.shape
    n_ctx = ctx.shape[1]
    depth = w_in.shape[0]
    rows = n // GRID_W
    assert bsz <= COND_ROWS // 2

    cond = jnp.zeros((COND_ROWS, D_MODEL), f32).at[:bsz].set(c).at[COND_ROWS // 2].set(c_ctx)
    mods = _ada_mods(cond, ada_w, ada_b).reshape(depth * COND_ROWS * 6, 1, D_MODEL)

    cos_l, sin_l = _rope_tables(n)
    rope_lat = (cos_l * DIFF_Q_SCALE, sin_l * DIFF_Q_SCALE, cos_l, sin_l)
    ones = jnp.ones((n_ctx, GROUP_W), f32)
    rope_ctx = (ones * DIFF_Q_SCALE, ones * 0.0, ones, ones * 0.0)
    bd_lat, bd_ctx = _channel_dft(n), _channel_dft(n_ctx)
    dft_lat, dft_ctx = _position_dft(n), _position_dft(n_ctx)

    w_in_b, w_out_b = w_in.astype(bf16), w_out.astype(bf16)
    w1_b, w2_b, sgu_w_b = w_ff1.astype(bf16), w_ff2.astype(bf16), sgu_w.astype(bf16)
    fg = final_g.reshape(1, D_MODEL)
    cx = ctx

    for l in range(depth):
        last = l == depth - 1
        lam_init = 0.8 - 0.6 * math.exp(-0.3 * l)
        lam = (jnp.exp(jnp.sum(diff_lq1[l].astype(f32) * diff_lk1[l].astype(f32)))
               - jnp.exp(jnp.sum(diff_lq2[l].astype(f32) * diff_lk2[l].astype(f32))) + lam_init).reshape(1)
        g1 = norm1_g[l].reshape(1, D_MODEL)
        g2 = norm2_g[l].reshape(1, D_MODEL)
        lng = sgu_ln_g[l].reshape(1, GROUP_W)
        lnb = sgu_ln_b[l].reshape(1, GROUP_W)
        sgu_bias = jnp.repeat(sgu_b[l].T, HEAD_DIM, axis=1)
        subln = lambda width: jnp.broadcast_to(diff_subln_g[l].reshape(HEAD_DIM, 1), (HEAD_DIM, width))
        post_scale = 1.0 - lam_init

        p_lat = _proj_in(x, mods, l, False, g1, w_in_b[l], bd_lat, rope_lat, lng, lnb)
        p_ctx = _proj_in(cx, mods, l, True, g1, w_in_b[l], bd_ctx, rope_ctx, lng, lnb)
        acs, naq, nak, nav, sgu_u, sgu_v, dfq, dfk, dfv = p_lat
        c_acs, c_naq, c_nak, c_nav, c_sgu_u, c_sgu_v, c_dfq, c_dfk, c_dfv = p_ctx

        y_a = _fourier(*dft_lat, acs)
        y_b = _na(naq, nak, nav, c_nak, c_nav, _na_bias(na_rpb[l], rows))
        y_c = _sgu(sgu_u, sgu_v, sgu_w_b[l], sgu_bias)
        k_all = _key_chunks(jnp.concatenate([dfk, c_dfk], axis=1))
        v_all = _value_chunks(jnp.concatenate([dfv, c_dfv], axis=1))
        y_d = _attn_t(dfq, k_all, v_all, lam, subln(min(n, 512)), 2, post_scale)

        if not last:
            yc_a = _fourier(*dft_ctx, c_acs)
            yc_b = _attn_t(c_naq, _key_chunks(c_nak), _value_chunks(c_nav), lam, subln(n_ctx), 1, 1.0)
            yc_c = _sgu(c_sgu_u, c_sgu_v, sgu_w_b[l], sgu_bias)
            yc_d = _attn_t(c_dfq, _key_chunks(c_dfk), _value_chunks(c_dfv), lam, subln(n_ctx), 2, post_scale)
            cx = _post(cx, (yc_a, yc_b, yc_c, yc_d), mods, l, True, g2, w_out_b[l], w1_b[l], w2_b[l], fg, False)

        x = _post(x, (y_a, y_b, y_c, y_d), mods, l, False, g2, w_out_b[l], w1_b[l], w2_b[l], fg, last)
    return x
```

```python
import math
from functools import partial

import numpy as np
import jax
import jax.numpy as jnp
from jax import lax
from jax.experimental import pallas as pl
from jax.experimental.pallas import tpu as pltpu

D_MODEL = 1024
GRID_W = 64
N_HEADS = 4
GROUP_W = 256
HEAD_DIM = 64
NA_WIN_ROWS = 8
NA_WIN_COLS = 16
SGU_CHUNK = 128
DIFF_QK_DIM = 32
ROPE_BASE = 10000.0
D_FF = 4 * D_MODEL
D_IN = 9 * GROUP_W
EPS = 1e-6

LOG2E = 1.4426950408889634
NA_Q_SCALE = HEAD_DIM ** -0.5 * LOG2E
DIFF_Q_SCALE = DIFF_QK_DIM ** -0.5 * LOG2E
MASKED = -1e30
COND_ROWS = 16
KEY_CHUNK = 256
NA_Q_ROWS = 8
NA_BAND_ROWS = 16
V7X_VMEM_LIMIT = 56 * 1024 * 1024

f32 = jnp.float32
bf16 = jnp.bfloat16


def _dot(a, b):
    return jnp.dot(a, b, preferred_element_type=f32)


def _dot_nt(a, b):
    return lax.dot_general(a, b, (((1,), (1,)), ((), ())), preferred_element_type=f32)


def _ada_kernel(cond_ref, w_ref, b_ref, o_ref):
    c = cond_ref[...]
    s = (c * jax.nn.sigmoid(c)).astype(bf16)
    o_ref[0] = _dot(s, w_ref[0].astype(bf16)) + b_ref[0]


def _ada_mods(cond, ada_w, ada_b):
    depth = ada_w.shape[0]
    tn = 1536
    return pl.pallas_call(
        _ada_kernel,
        grid=(depth, 6 * D_MODEL // tn),
        in_specs=[
            pl.BlockSpec((COND_ROWS, D_MODEL), lambda l, j: (0, 0)),
            pl.BlockSpec((1, D_MODEL, tn), lambda l, j: (l, 0, j)),
            pl.BlockSpec((1, 1, tn), lambda l, j: (l, 0, j)),
        ],
        out_specs=pl.BlockSpec((1, COND_ROWS, tn), lambda l, j: (l, 0, j)),
        out_shape=jax.ShapeDtypeStruct((depth, COND_ROWS, 6 * D_MODEL), f32),
        compiler_params=pltpu.CompilerParams(vmem_limit_bytes=V7X_VMEM_LIMIT),
        name="ada",
    )(cond, ada_w, ada_b.reshape(depth, 1, 6 * D_MODEL))


def _mod_spec(layer, which, ctx):
    if ctx:
        return pl.BlockSpec((1, 1, D_MODEL), lambda b, i: ((layer * COND_ROWS + COND_ROWS // 2) * 6 + which, 0, 0))
    return pl.BlockSpec((1, 1, D_MODEL), lambda b, i: ((layer * COND_ROWS + b) * 6 + which, 0, 0))


def _rope_partner(x):
    lane = lax.broadcasted_iota(jnp.int32, x.shape, 1)
    first_half = (lane % 16) < 8
    n = x.shape[1]
    return jnp.where(first_half, pltpu.roll(x, n - 8, 1), pltpu.roll(x, 8, 1))


def _proj_in_kernel(x_ref, sh_ref, sc_ref, g_ref, w_ref, bd_ref, cq_ref, sq_ref, ck_ref, sk_ref, lng_ref, lnb_ref,
                    acs_ref, naq_ref, nak_ref, nav_ref, sgu_ref, sgv_ref, dfq_ref, dfk_ref, dfv_ref):
    x = x_ref[0]
    r = lax.rsqrt(jnp.mean(x * x, axis=-1, keepdims=True) + EPS)
    h = (x * r) * g_ref[...]
    h = h * (1.0 + sc_ref[0]) + sh_ref[0]
    hb = h.astype(bf16)
    gw = GROUP_W

    def mm(col, width=gw):
        return _dot(hb, w_ref[:, col:col + width])

    acs_ref[0] = _dot(mm(0).astype(bf16), bd_ref[...]).astype(bf16)
    naq_ref[0] = (mm(gw) * NA_Q_SCALE).astype(bf16)
    nak_ref[0] = mm(2 * gw).astype(bf16)
    nav_ref[0] = mm(3 * gw).astype(bf16)
    uv = jax.nn.gelu(mm(4 * gw, 2 * gw), approximate=True)
    sgu_ref[0] = uv[:, :gw].astype(bf16)
    v = uv[:, gw:]
    mu = jnp.mean(v, axis=-1, keepdims=True)
    vc = v - mu
    var = jnp.mean(vc * vc, axis=-1, keepdims=True)
    sgv_ref[0] = (vc * lax.rsqrt(var + EPS) * lng_ref[...] + lnb_ref[...]).astype(bf16)
    q = mm(6 * gw)
    dfq_ref[0] = (q * cq_ref[...] + _rope_partner(q) * sq_ref[...]).astype(bf16)
    k = mm(7 * gw)
    dfk_ref[0] = (k * ck_ref[...] + _rope_partner(k) * sk_ref[...]).astype(bf16)
    dfv_ref[0] = mm(8 * gw).astype(bf16)


def _proj_in(x, mods, layer, ctx, g, w, bd, rope, lng, lnb):
    bsz, s, _ = x.shape
    tm = min(s, 512)
    row = lambda b, i: (b, i, 0)
    const = lambda b, i: (0, 0)
    pos = lambda b, i: (i, 0)
    out = lambda width: pl.BlockSpec((1, tm, width), row)
    shape = lambda width: jax.ShapeDtypeStruct((bsz, s, width), bf16)
    widths = (2 * GROUP_W,) + (GROUP_W,) * 8
    return pl.pallas_call(
        _proj_in_kernel,
        grid=(bsz, s // tm),
        in_specs=[
            pl.BlockSpec((1, tm, D_MODEL), row),
            _mod_spec(layer, 0, ctx), _mod_spec(layer, 1, ctx),
            pl.BlockSpec((1, D_MODEL), const),
            pl.BlockSpec((D_MODEL, D_IN), const),
            pl.BlockSpec((GROUP_W, 2 * GROUP_W), const),
            pl.BlockSpec((tm, GROUP_W), pos), pl.BlockSpec((tm, GROUP_W), pos),
            pl.BlockSpec((tm, GROUP_W), pos), pl.BlockSpec((tm, GROUP_W), pos),
            pl.BlockSpec((1, GROUP_W), const), pl.BlockSpec((1, GROUP_W), const),
        ],
        out_specs=[out(wd) for wd in widths],
        out_shape=[shape(wd) for wd in widths],
        compiler_params=pltpu.CompilerParams(vmem_limit_bytes=V7X_VMEM_LIMIT),
        name="proj_in_ctx" if ctx else "proj_in",
    )(x, mods, mods, g, w, bd, *rope, lng, lnb)


def _fourier_kernel(c_ref, s_ref, a_ref, o_ref, acc_ref):
    k = pl.program_id(1)
    gw = GROUP_W

    @pl.when(k == 0)
    def _():
        acc_ref[...] = jnp.zeros_like(acc_ref)

    for b in range(a_ref.shape[0]):
        acc_ref[b] += _dot(c_ref[...], a_ref[b, :, :gw]) + _dot(s_ref[...], a_ref[b, :, gw:])

    @pl.when(k == pl.num_programs(1) - 1)
    def _():
        o_ref[...] = acc_ref[...].astype(o_ref.dtype)


def _fourier(cn, sn, acs):
    bsz, s, _ = acs.shape
    tm = min(s, 1024)
    tk = min(s, 512)
    return pl.pallas_call(
        _fourier_kernel,
        grid=(s // tm, s // tk),
        in_specs=[
            pl.BlockSpec((tm, tk), lambda i, k: (i, k)),
            pl.BlockSpec((tm, tk), lambda i, k: (i, k)),
            pl.BlockSpec((bsz, tk, 2 * GROUP_W), lambda i, k: (0, k, 0)),
        ],
        out_specs=pl.BlockSpec((bsz, tm, GROUP_W), lambda i, k: (0, i, 0)),
        out_shape=jax.ShapeDtypeStruct((bsz, s, GROUP_W), bf16),
        scratch_shapes=[pltpu.VMEM((bsz, tm, GROUP_W), f32)],
        compiler_params=pltpu.CompilerParams(vmem_limit_bytes=V7X_VMEM_LIMIT),
        name="fourier",
    )(cn, sn, acs)


def _na_band_start(i):
    rows = GRID_W
    return jnp.clip(i * NA_Q_ROWS - NA_WIN_ROWS // 2, 0, rows - NA_BAND_ROWS)


def _na_kernel(q_ref, k_ref, v_ref, kc_ref, vc_ref, bias_ref, o_ref):
    i = pl.program_id(0)
    start = pl.multiple_of(_na_band_start(i) * GRID_W, 256)
    kb = k_ref[0, pl.ds(start, NA_BAND_ROWS * GRID_W), :]
    vb = v_ref[0, pl.ds(start, NA_BAND_ROWS * GRID_W), :]
    kc = kc_ref[0]
    vc = vc_ref[0]
    q = q_ref[0]
    lane = lax.broadcasted_iota(jnp.int32, q.shape, 1)
    out = jnp.zeros(q.shape, f32)
    for h in range(N_HEADS):
        in_head = (lane >= h * HEAD_DIM) & (lane < (h + 1) * HEAD_DIM)
        qh = jnp.where(in_head, q, jnp.zeros_like(q))
        s_loc = _dot_nt(qh, kb) + bias_ref[0, h]
        s_ctx = _dot_nt(qh, kc)
        m = jnp.maximum(jnp.max(s_loc, axis=-1, keepdims=True), jnp.max(s_ctx, axis=-1, keepdims=True))
        p_loc = jnp.exp2(s_loc - m)
        p_ctx = jnp.exp2(s_ctx - m)
        l = jnp.sum(p_loc, axis=-1, keepdims=True) + jnp.sum(p_ctx, axis=-1, keepdims=True)
        o = _dot(p_loc.astype(bf16), vb) + _dot(p_ctx.astype(bf16), vc)
        out = jnp.where(in_head, o / l, out)
    o_ref[0] = out.astype(o_ref.dtype)


def _na(q, k, v, kc, vc, bias):
    bsz, s, _ = q.shape
    tq = NA_Q_ROWS * GRID_W
    nblk = s // tq
    band = NA_BAND_ROWS * GRID_W

    def bias_map(i, b):
        return (jnp.where(i == 0, 0, jnp.where(i == nblk - 1, 2, 1)), 0, 0, 0)

    full = lambda n: pl.BlockSpec((1, n, GROUP_W), lambda i, b: (b, 0, 0))
    return pl.pallas_call(
        _na_kernel,
        grid=(nblk, bsz),
        in_specs=[
            pl.BlockSpec((1, tq, GROUP_W), lambda i, b: (b, i, 0)),
            full(s), full(s), full(kc.shape[1]), full(kc.shape[1]),
            pl.BlockSpec((1, N_HEADS, tq, band), bias_map),
        ],
        out_specs=pl.BlockSpec((1, tq, GROUP_W), lambda i, b: (b, i, 0)),
        out_shape=jax.ShapeDtypeStruct((bsz, s, GROUP_W), bf16),
        compiler_params=pltpu.CompilerParams(vmem_limit_bytes=V7X_VMEM_LIMIT),
        name="na",
    )(q, k, v, kc, vc, bias)


def _na_bias(rpb, rows):
    cidx = np.arange(GRID_W)
    col_start = np.clip(cidx - NA_WIN_COLS // 2, 0, GRID_W - NA_WIN_COLS)
    in_win = (cidx[None, :] >= col_start[:, None]) & (cidx[None, :] < col_start[:, None] + NA_WIN_COLS)
    dc = np.clip(cidx[None, :] - cidx[:, None], 1 - NA_WIN_COLS, NA_WIN_COLS - 1) + NA_WIN_COLS - 1
    t = jnp.where(in_win[None, None], rpb[:, :, dc] * LOG2E, MASKED)
    t = jnp.concatenate([t, jnp.full_like(t[:, :1], MASKED)], axis=1)
    nblk = rows // NA_Q_ROWS
    dr_idx = np.full((3, NA_Q_ROWS, NA_BAND_ROWS), 2 * NA_WIN_ROWS - 1, np.int32)
    for cat, blk in enumerate((0, 1, nblk - 1)):
        band0 = int(np.clip(blk * NA_Q_ROWS - NA_WIN_ROWS // 2, 0, rows - NA_BAND_ROWS))
        for qi in range(NA_Q_ROWS):
            r = blk * NA_Q_ROWS + qi
            r0 = int(np.clip(r - NA_WIN_ROWS // 2, 0, rows - NA_WIN_ROWS))
            for kj in range(NA_BAND_ROWS):
                kr = band0 + kj
                if r0 <= kr < r0 + NA_WIN_ROWS:
                    dr_idx[cat, qi, kj] = kr - r + NA_WIN_ROWS - 1
    big = t[:, dr_idx]
    big = big.transpose(1, 0, 2, 4, 3, 5)
    return big.reshape(3, N_HEADS, NA_Q_ROWS * GRID_W, NA_BAND_ROWS * GRID_W)


def _sgu_kernel(u_ref, v_ref, w_ref, b_ref, o_ref):
    tm = u_ref.shape[1]
    lane = lax.broadcasted_iota(jnp.int32, (SGU_CHUNK, GROUP_W), 1)
    for c in range(tm // SGU_CHUNK):
        rows = slice(c * SGU_CHUNK, (c + 1) * SGU_CHUNK)
        vchunk = v_ref[0, rows, :]
        s = b_ref[...]
        for g in range(N_HEADS):
            in_group = (lane >= g * HEAD_DIM) & (lane < (g + 1) * HEAD_DIM)
            s = jnp.where(in_group, s + _dot(w_ref[g], vchunk), s)
        o_ref[0, rows, :] = (u_ref[0, rows, :].astype(f32) * s).astype(o_ref.dtype)


def _sgu(u, v, w, b_full):
    bsz, s, _ = u.shape
    tm = min(s, 512)
    row = pl.BlockSpec((1, tm, GROUP_W), lambda b, i: (b, i, 0))
    return pl.pallas_call(
        _sgu_kernel,
        grid=(bsz, s // tm),
        in_specs=[row, row,
                  pl.BlockSpec((N_HEADS, SGU_CHUNK, SGU_CHUNK), lambda b, i: (0, 0, 0)),
                  pl.BlockSpec((SGU_CHUNK, GROUP_W), lambda b, i: (0, 0))],
        out_specs=row,
        out_shape=jax.ShapeDtypeStruct((bsz, s, GROUP_W), bf16),
        name="sgu",
    )(u, v, w, b_full)


def _attn_t_kernel(lam_ref, q_ref, k_ref, vt_ref, g_ref, o_ref, qm_ref, sa_ref, sb_ref, ma_ref, mb_ref, acc_ref,
                   *, n_maps, post_scale):
    n_streams = N_HEADS * n_maps
    width = GROUP_W // n_streams
    q = q_ref[0]
    lane = lax.broadcasted_iota(jnp.int32, q.shape, 1)
    for j in range(n_streams):
        qm_ref[j] = jnp.where((lane >= j * width) & (lane < (j + 1) * width), q, jnp.zeros_like(q))

    tq = q.shape[0]
    n_chunks = k_ref.shape[1] // KEY_CHUNK

    def step(j_scores, j_values, to_first):
        s_new, m_new = (sa_ref, ma_ref) if to_first else (sb_ref, mb_ref)
        s_old, m_old = (sb_ref, mb_ref) if to_first else (sa_ref, ma_ref)
        if j_scores is not None:
            mx = jnp.full((8, tq), MASKED, f32)
        if j_values is not None:
            m = m_old[...]
            acc = jnp.zeros((2 * HEAD_DIM, tq), f32)
        for c in range(n_chunks):
            keys = slice(c * KEY_CHUNK, (c + 1) * KEY_CHUNK)
            if j_scores is not None:
                s = _dot_nt(k_ref[0, keys, :], qm_ref[j_scores])
                s_new[keys, :] = s
                mx = jnp.maximum(mx, jnp.max(s.reshape(KEY_CHUNK // 8, 8, tq), axis=0))
            if j_values is not None:
                p = jnp.exp2(s_old[keys, :] - m).astype(bf16)
                acc = acc + _dot(vt_ref[0, j_values // n_maps, :, keys], p)
        if j_scores is not None:
            m_new[...] = jnp.max(mx, axis=0, keepdims=True)
        if j_values is not None:
            acc_ref[j_values] = acc

    step(0, None, True)

    def stream_pair(i, carry):
        step(2 * i + 1, 2 * i, False)
        step(2 * i + 2, 2 * i + 1, True)
        return carry

    lax.fori_loop(0, n_streams // 2 - 1, stream_pair, 0)
    step(n_streams - 1, n_streams - 2, False)
    step(None, n_streams - 1, True)

    outs = []
    for h in range(N_HEADS):
        if n_maps == 1:
            a = acc_ref[h]
            o = a[:HEAD_DIM] / a[HEAD_DIM:HEAD_DIM + 1]
        else:
            a1 = acc_ref[2 * h]
            a2 = acc_ref[2 * h + 1]
            o = a1[:HEAD_DIM] / a1[HEAD_DIM:HEAD_DIM + 1] - lam_ref[0] * (a2[:HEAD_DIM] / a2[HEAD_DIM:HEAD_DIM + 1])
            r = lax.rsqrt(jnp.mean(o * o, axis=0, keepdims=True) + EPS)
            o = (o * r) * g_ref[...] * post_scale
        outs.append(o)
    o_ref[0] = jnp.concatenate(outs, axis=0).T.astype(o_ref.dtype)


def _attn_t(q, k, vt, lam, g_rows, n_maps, post_scale):
    bsz, s, _ = q.shape
    tq = min(s, 512)
    n_keys = k.shape[1]
    n_streams = N_HEADS * n_maps
    return pl.pallas_call(
        partial(_attn_t_kernel, n_maps=n_maps, post_scale=post_scale),
        grid=(bsz, s // tq),
        in_specs=[
            pl.BlockSpec(memory_space=pltpu.SMEM),
            pl.BlockSpec((1, tq, GROUP_W), lambda b, i: (b, i, 0)),
            pl.BlockSpec((1, n_keys, GROUP_W), lambda b, i: (b, 0, 0)),
            pl.BlockSpec((1, N_HEADS, 2 * HEAD_DIM, n_keys), lambda b, i: (b, 0, 0, 0)),
            pl.BlockSpec((HEAD_DIM, tq), lambda b, i: (0, 0)),
        ],
        out_specs=pl.BlockSpec((1, tq, GROUP_W), lambda b, i: (b, i, 0)),
        out_shape=jax.ShapeDtypeStruct((bsz, s, GROUP_W), bf16),
        scratch_shapes=[
            pltpu.VMEM((n_streams, tq, GROUP_W), bf16),
            pltpu.VMEM((n_keys, tq), f32), pltpu.VMEM((n_keys, tq), f32),
            pltpu.VMEM((1, tq), f32), pltpu.VMEM((1, tq), f32),
            pltpu.VMEM((n_streams, 2 * HEAD_DIM, tq), f32),
        ],
        compiler_params=pltpu.CompilerParams(vmem_limit_bytes=V7X_VMEM_LIMIT),
        name="attn_t%d" % n_maps,
    )(lam, q, k, vt, g_rows)


def _values_t(v):
    bsz, n, _ = v.shape
    vt = v.reshape(bsz, n, N_HEADS, HEAD_DIM).transpose(0, 2, 3, 1)
    return jnp.concatenate([vt, jnp.ones_like(vt)], axis=2)


def _post_kernel(x_ref, ya_ref, yb_ref, yc_ref, yd_ref, g1_ref, sh_ref, sc_ref, g2_ref, ng_ref, wo_ref, w1_ref,
                 w2_ref, fg_ref, o_ref, *, final):
    y = jnp.concatenate([ya_ref[0], yb_ref[0], yc_ref[0], yd_ref[0]], axis=-1)
    x = x_ref[0] + g1_ref[0] * _dot(y, wo_ref[...])
    r = lax.rsqrt(jnp.mean(x * x, axis=-1, keepdims=True) + EPS)
    h = (x * r) * ng_ref[...]
    hb = (h * (1.0 + sc_ref[0]) + sh_ref[0]).astype(bf16)
    acc = jnp.zeros(x.shape, f32)
    step = D_MODEL
    for j in range(D_FF // step):
        t = jnp.maximum(_dot(hb, w1_ref[:, j * step:(j + 1) * step]), 0.0)
        acc = acc + _dot((t * t).astype(bf16), w2_ref[j * step:(j + 1) * step, :])
    x = x + g2_ref[0] * acc
    if final:
        r = lax.rsqrt(jnp.mean(x * x, axis=-1, keepdims=True) + EPS)
        x = (x * r) * fg_ref[...]
    o_ref[0] = x


def _post(x, ys, mods, layer, ctx, ng, wo, w1, w2, fg, final):
    bsz, s, _ = x.shape
    tm = min(s, 512)
    row = lambda width: pl.BlockSpec((1, tm, width), lambda b, i: (b, i, 0))
    const = lambda shape: pl.BlockSpec(shape, lambda b, i: (0, 0), pipeline_mode=pl.Buffered(1))
    return pl.pallas_call(
        partial(_post_kernel, final=final),
        grid=(bsz, s // tm),
        in_specs=[
            row(D_MODEL), row(GROUP_W), row(GROUP_W), row(GROUP_W), row(GROUP_W),
            _mod_spec(layer, 2, ctx), _mod_spec(layer, 3, ctx), _mod_spec(layer, 4, ctx), _mod_spec(layer, 5, ctx),
            const((1, D_MODEL)), const((D_MODEL, D_MODEL)), const((D_MODEL, D_FF)), const((D_FF, D_MODEL)),
            const((1, D_MODEL)),
        ],
        out_specs=row(D_MODEL),
        out_shape=jax.ShapeDtypeStruct((bsz, s, D_MODEL), f32),
        compiler_params=pltpu.CompilerParams(vmem_limit_bytes=V7X_VMEM_LIMIT),
        name="post_ctx" if ctx else "post",
    )(x, *ys, mods, mods, mods, mods, ng, wo, w1, w2, fg)


def _channel_dft(n_pos):
    c = np.arange(HEAD_DIM)
    ang = 2.0 * np.pi * ((c[:, None] * c[None, :]) % HEAD_DIM) / HEAD_DIM
    scale = 1.0 / math.sqrt(n_pos * HEAD_DIM)
    bd = np.zeros((GROUP_W, 2 * GROUP_W), np.float32)
    for g in range(N_HEADS):
        sl = slice(g * HEAD_DIM, (g + 1) * HEAD_DIM)
        bd[sl, sl] = np.cos(ang) * scale
        bd[sl, GROUP_W + g * HEAD_DIM:GROUP_W + (g + 1) * HEAD_DIM] = -np.sin(ang) * scale
    return jnp.asarray(bd, bf16)


def _position_dft(n_pos):
    hi = max(n_pos // GRID_W, 1)
    lo = n_pos // hi
    np_ = jnp.arange(n_pos, dtype=jnp.int32)[:, None]
    a = 2.0 * np.pi * ((np_ * jnp.arange(hi, dtype=jnp.int32)[None, :] * lo) % n_pos).astype(f32) / n_pos
    b = 2.0 * np.pi * ((np_ * jnp.arange(lo, dtype=jnp.int32)[None, :]) % n_pos).astype(f32) / n_pos
    ca, sa, cb, sb = jnp.cos(a)[:, :, None], jnp.sin(a)[:, :, None], jnp.cos(b)[:, None, :], jnp.sin(b)[:, None, :]
    cn = (ca * cb - sa * sb).reshape(n_pos, n_pos)
    sn = (sa * cb + ca * sb).reshape(n_pos, n_pos)
    return cn.astype(bf16), sn.astype(bf16)


def _rope_tables(n):
    t = jnp.arange(n)
    rows = (t // GRID_W).astype(f32)
    cols = (t % GRID_W).astype(f32)
    n_freq = DIFF_QK_DIM // 4
    inv = ROPE_BASE ** (-jnp.arange(n_freq, dtype=f32) / n_freq)
    ang = jnp.concatenate([rows[:, None] * inv, cols[:, None] * inv], axis=-1)
    lane = np.arange(GROUP_W) % DIFF_QK_DIM
    src = (lane // 16) * n_freq + lane % n_freq
    sign = np.where((lane % 16) < n_freq, -1.0, 1.0).astype(np.float32)
    return jnp.cos(ang)[:, src], jnp.sin(ang)[:, src] * sign


def kernel(x, c, ctx, c_ctx, ada_w, ada_b, norm1_g, norm2_g, w_in, w_out, na_rpb, sgu_ln_g, sgu_ln_b, sgu_w, sgu_b,
           diff_lq1, diff_lk1, diff_lq2, diff_lk2, diff_subln_g, w_ff1, w_ff2, final_g):
    bsz, n, _ = x.shape
    n_ctx = ctx.shape[1]
    depth = w_in.shape[0]
    rows = n // GRID_W
    assert bsz <= COND_ROWS // 2

    cond = jnp.zeros((COND_ROWS, D_MODEL), f32).at[:bsz].set(c).at[COND_ROWS // 2].set(c_ctx)
    mods = _ada_mods(cond, ada_w, ada_b).reshape(depth * COND_ROWS * 6, 1, D_MODEL)

    cos_l, sin_l = _rope_tables(n)
    rope_lat = (cos_l * DIFF_Q_SCALE, sin_l * DIFF_Q_SCALE, cos_l, sin_l)
    ones = jnp.ones((n_ctx, GROUP_W), f32)
    rope_ctx = (ones * DIFF_Q_SCALE, ones * 0.0, ones, ones * 0.0)
    bd_lat, bd_ctx = _channel_dft(n), _channel_dft(n_ctx)
    dft_lat, dft_ctx = _position_dft(n), _position_dft(n_ctx)

    w_in_b, w_out_b = w_in.astype(bf16), w_out.astype(bf16)
    w1_b, w2_b, sgu_w_b = w_ff1.astype(bf16), w_ff2.astype(bf16), sgu_w.astype(bf16)
    fg = final_g.reshape(1, D_MODEL)
    cx = ctx

    for l in range(depth):
        last = l == depth - 1
        lam_init = 0.8 - 0.6 * math.exp(-0.3 * l)
        lam = (jnp.exp(jnp.sum(diff_lq1[l].astype(f32) * diff_lk1[l].astype(f32)))
               - jnp.exp(jnp.sum(diff_lq2[l].astype(f32) * diff_lk2[l].astype(f32))) + lam_init).reshape(1)
        g1 = norm1_g[l].reshape(1, D_MODEL)
        g2 = norm2_g[l].reshape(1, D_MODEL)
        lng = sgu_ln_g[l].reshape(1, GROUP_W)
        lnb = sgu_ln_b[l].reshape(1, GROUP_W)
        sgu_bias = jnp.repeat(sgu_b[l].T, HEAD_DIM, axis=1)
        subln = lambda width: jnp.broadcast_to(diff_subln_g[l].reshape(HEAD_DIM, 1), (HEAD_DIM, width))
        post_scale = 1.0 - lam_init

        p_lat = _proj_in(x, mods, l, False, g1, w_in_b[l], bd_lat, rope_lat, lng, lnb)
        p_ctx = _proj_in(cx, mods, l, True, g1, w_in_b[l], bd_ctx, rope_ctx, lng, lnb)
        acs, naq, nak, nav, sgu_u, sgu_v, dfq, dfk, dfv = p_lat
        c_acs, c_naq, c_nak, c_nav, c_sgu_u, c_sgu_v, c_dfq, c_dfk, c_dfv = p_ctx

        y_a = _fourier(*dft_lat, acs)
        y_b = _na(naq, nak, nav, c_nak, c_nav, _na_bias(na_rpb[l], rows))
        y_c = _sgu(sgu_u, sgu_v, sgu_w_b[l], sgu_bias)
        k_all = jnp.concatenate([dfk, c_dfk], axis=1)
        v_all = _values_t(jnp.concatenate([dfv, c_dfv], axis=1))
        y_d = _attn_t(dfq, k_all, v_all, lam, subln(min(n, 512)), 2, post_scale)

        if not last:
            yc_a = _fourier(*dft_ctx, c_acs)
            yc_b = _attn_t(c_naq, c_nak, _values_t(c_nav), lam, subln(n_ctx), 1, 1.0)
            yc_c = _sgu(c_sgu_u, c_sgu_v, sgu_w_b[l], sgu_bias)
            yc_d = _attn_t(c_dfq, c_dfk, _values_t(c_dfv), lam, subln(n_ctx), 2, post_scale)
            cx = _post(cx, (yc_a, yc_b, yc_c, yc_d), mods, l, True, g2, w_out_b[l], w1_b[l], w2_b[l], fg, False)

        x = _post(x, (y_a, y_b, y_c, y_d), mods, l, False, g2, w_out_b[l], w1_b[l], w2_b[l], fg, last)
    return x
```

```python
import math
from functools import partial

import numpy as np
import jax
import jax.numpy as jnp
from jax import lax
from jax.experimental import pallas as pl
from jax.experimental.pallas import tpu as pltpu

D_MODEL = 1024
GRID_W = 64
N_HEADS = 4
GROUP_W = 256
HEAD_DIM = 64
NA_WIN_ROWS = 8
NA_WIN_COLS = 16
SGU_CHUNK = 128
DIFF_QK_DIM = 32
ROPE_BASE = 10000.0
D_FF = 4 * D_MODEL
D_IN = 9 * GROUP_W
EPS = 1e-6

LOG2E = 1.4426950408889634
NA_Q_SCALE = HEAD_DIM ** -0.5 * LOG2E
DIFF_Q_SCALE = DIFF_QK_DIM ** -0.5 * LOG2E
MASKED = -1e30
COND_ROWS = 16
KEY_CHUNK = 256
NA_Q_ROWS = 8
NA_BAND_ROWS = 16
V7X_VMEM_LIMIT = 56 * 1024 * 1024

f32 = jnp.float32
bf16 = jnp.bfloat16


def _dot(a, b):
    return jnp.dot(a, b, preferred_element_type=f32)


def _dot_nt(a, b):
    return lax.dot_general(a, b, (((1,), (1,)), ((), ())), preferred_element_type=f32)


def _ada_kernel(cond_ref, w_ref, b_ref, o_ref):
    c = cond_ref[...]
    s = (c * jax.nn.sigmoid(c)).astype(bf16)
    o_ref[0] = _dot(s, w_ref[0].astype(bf16)) + b_ref[0]


def _ada_mods(cond, ada_w, ada_b):
    depth = ada_w.shape[0]
    tn = 1536
    return pl.pallas_call(
        _ada_kernel,
        grid=(depth, 6 * D_MODEL // tn),
        in_specs=[
            pl.BlockSpec((COND_ROWS, D_MODEL), lambda l, j: (0, 0)),
            pl.BlockSpec((1, D_MODEL, tn), lambda l, j: (l, 0, j)),
            pl.BlockSpec((1, 1, tn), lambda l, j: (l, 0, j)),
        ],
        out_specs=pl.BlockSpec((1, COND_ROWS, tn), lambda l, j: (l, 0, j)),
        out_shape=jax.ShapeDtypeStruct((depth, COND_ROWS, 6 * D_MODEL), f32),
        compiler_params=pltpu.CompilerParams(vmem_limit_bytes=V7X_VMEM_LIMIT),
        name="ada",
    )(cond, ada_w, ada_b.reshape(depth, 1, 6 * D_MODEL))


def _mod_spec(layer, which, ctx):
    if ctx:
        return pl.BlockSpec((1, 1, D_MODEL), lambda b, i: ((layer * COND_ROWS + COND_ROWS // 2) * 6 + which, 0, 0))
    return pl.BlockSpec((1, 1, D_MODEL), lambda b, i: ((layer * COND_ROWS + b) * 6 + which, 0, 0))


def _rope_partner(x):
    lane = lax.broadcasted_iota(jnp.int32, x.shape, 1)
    first_half = (lane % 16) < 8
    n = x.shape[1]
    return jnp.where(first_half, pltpu.roll(x, n - 8, 1), pltpu.roll(x, 8, 1))


def _proj_in_kernel(x_ref, sh_ref, sc_ref, g_ref, w_ref, bd_ref, cq_ref, sq_ref, ck_ref, sk_ref, lng_ref, lnb_ref,
                    acs_ref, naq_ref, nak_ref, nav_ref, sgu_ref, sgv_ref, dfq_ref, dfk_ref, dfv_ref):
    x = x_ref[0]
    r = lax.rsqrt(jnp.mean(x * x, axis=-1, keepdims=True) + EPS)
    h = (x * r) * g_ref[...]
    h = h * (1.0 + sc_ref[0]) + sh_ref[0]
    hb = h.astype(bf16)
    gw = GROUP_W

    def mm(col, width=gw):
        return _dot(hb, w_ref[:, col:col + width])

    acs_ref[0] = _dot(mm(0).astype(bf16), bd_ref[...]).astype(bf16)
    naq_ref[0] = (mm(gw) * NA_Q_SCALE).astype(bf16)
    nak_ref[0] = mm(2 * gw).astype(bf16)
    nav_ref[0] = mm(3 * gw).astype(bf16)
    uv = jax.nn.gelu(mm(4 * gw, 2 * gw), approximate=True)
    sgu_ref[0] = uv[:, :gw].astype(bf16)
    v = uv[:, gw:]
    mu = jnp.mean(v, axis=-1, keepdims=True)
    vc = v - mu
    var = jnp.mean(vc * vc, axis=-1, keepdims=True)
    sgv_ref[0] = (vc * lax.rsqrt(var + EPS) * lng_ref[...] + lnb_ref[...]).astype(bf16)
    q = mm(6 * gw)
    dfq_ref[0] = (q * cq_ref[...] + _rope_partner(q) * sq_ref[...]).astype(bf16)
    k = mm(7 * gw)
    dfk_ref[0] = (k * ck_ref[...] + _rope_partner(k) * sk_ref[...]).astype(bf16)
    dfv_ref[0] = mm(8 * gw).astype(bf16)


def _proj_in(x, mods, layer, ctx, g, w, bd, rope, lng, lnb):
    bsz, s, _ = x.shape
    tm = min(s, 512)
    row = lambda b, i: (b, i, 0)
    const = lambda b, i: (0, 0)
    pos = lambda b, i: (i, 0)
    out = lambda width: pl.BlockSpec((1, tm, width), row)
    shape = lambda width: jax.ShapeDtypeStruct((bsz, s, width), bf16)
    widths = (2 * GROUP_W,) + (GROUP_W,) * 8
    return pl.pallas_call(
        _proj_in_kernel,
        grid=(bsz, s // tm),
        in_specs=[
            pl.BlockSpec((1, tm, D_MODEL), row),
            _mod_spec(layer, 0, ctx), _mod_spec(layer, 1, ctx),
            pl.BlockSpec((1, D_MODEL), const),
            pl.BlockSpec((D_MODEL, D_IN), const),
            pl.BlockSpec((GROUP_W, 2 * GROUP_W), const),
            pl.BlockSpec((tm, GROUP_W), pos), pl.BlockSpec((tm, GROUP_W), pos),
            pl.BlockSpec((tm, GROUP_W), pos), pl.BlockSpec((tm, GROUP_W), pos),
            pl.BlockSpec((1, GROUP_W), const), pl.BlockSpec((1, GROUP_W), const),
        ],
        out_specs=[out(wd) for wd in widths],
        out_shape=[shape(wd) for wd in widths],
        compiler_params=pltpu.CompilerParams(vmem_limit_bytes=V7X_VMEM_LIMIT),
        name="proj_in_ctx" if ctx else "proj_in",
    )(x, mods, mods, g, w, bd, *rope, lng, lnb)


def _fourier_kernel(c_ref, s_ref, a_ref, o_ref, acc_ref):
    k = pl.program_id(1)
    gw = GROUP_W

    @pl.when(k == 0)
    def _():
        acc_ref[...] = jnp.zeros_like(acc_ref)

    for b in range(a_ref.shape[0]):
        acc_ref[b] += _dot(c_ref[...], a_ref[b, :, :gw]) + _dot(s_ref[...], a_ref[b, :, gw:])

    @pl.when(k == pl.num_programs(1) - 1)
    def _():
        o_ref[...] = acc_ref[...].astype(o_ref.dtype)


def _fourier(cn, sn, acs):
    bsz, s, _ = acs.shape
    tm = min(s, 1024)
    tk = min(s, 512)
    return pl.pallas_call(
        _fourier_kernel,
        grid=(s // tm, s // tk),
        in_specs=[
            pl.BlockSpec((tm, tk), lambda i, k: (i, k)),
            pl.BlockSpec((tm, tk), lambda i, k: (i, k)),
            pl.BlockSpec((bsz, tk, 2 * GROUP_W), lambda i, k: (0, k, 0)),
        ],
        out_specs=pl.BlockSpec((bsz, tm, GROUP_W), lambda i, k: (0, i, 0)),
        out_shape=jax.ShapeDtypeStruct((bsz, s, GROUP_W), bf16),
        scratch_shapes=[pltpu.VMEM((bsz, tm, GROUP_W), f32)],
        compiler_params=pltpu.CompilerParams(vmem_limit_bytes=V7X_VMEM_LIMIT),
        name="fourier",
    )(cn, sn, acs)


def _na_band_start(i):
    rows = GRID_W
    return jnp.clip(i * NA_Q_ROWS - NA_WIN_ROWS // 2, 0, rows - NA_BAND_ROWS)


def _na_kernel(q_ref, k_ref, v_ref, kc_ref, vc_ref, bias_ref, o_ref):
    i = pl.program_id(0)
    start = pl.multiple_of(_na_band_start(i) * GRID_W, 256)
    kb = k_ref[0, pl.ds(start, NA_BAND_ROWS * GRID_W), :]
    vb = v_ref[0, pl.ds(start, NA_BAND_ROWS * GRID_W), :]
    kc = kc_ref[0]
    vc = vc_ref[0]
    q = q_ref[0]
    lane = lax.broadcasted_iota(jnp.int32, q.shape, 1)
    out = jnp.zeros(q.shape, f32)
    for h in range(N_HEADS):
        in_head = (lane >= h * HEAD_DIM) & (lane < (h + 1) * HEAD_DIM)
        qh = jnp.where(in_head, q, jnp.zeros_like(q))
        s_loc = _dot_nt(qh, kb) + bias_ref[0, h]
        s_ctx = _dot_nt(qh, kc)
        m = jnp.maximum(jnp.max(s_loc, axis=-1, keepdims=True), jnp.max(s_ctx, axis=-1, keepdims=True))
        p_loc = jnp.exp2(s_loc - m)
        p_ctx = jnp.exp2(s_ctx - m)
        l = jnp.sum(p_loc, axis=-1, keepdims=True) + jnp.sum(p_ctx, axis=-1, keepdims=True)
        o = _dot(p_loc.astype(bf16), vb) + _dot(p_ctx.astype(bf16), vc)
        out = jnp.where(in_head, o / l, out)
    o_ref[0] = out.astype(o_ref.dtype)


def _na(q, k, v, kc, vc, bias):
    bsz, s, _ = q.shape
    tq = NA_Q_ROWS * GRID_W
    nblk = s // tq
    band = NA_BAND_ROWS * GRID_W

    def bias_map(i, b):
        return (jnp.where(i == 0, 0, jnp.where(i == nblk - 1, 2, 1)), 0, 0, 0)

    full = lambda n: pl.BlockSpec((1, n, GROUP_W), lambda i, b: (b, 0, 0))
    return pl.pallas_call(
        _na_kernel,
        grid=(nblk, bsz),
        in_specs=[
            pl.BlockSpec((1, tq, GROUP_W), lambda i, b: (b, i, 0)),
            full(s), full(s), full(kc.shape[1]), full(kc.shape[1]),
            pl.BlockSpec((1, N_HEADS, tq, band), bias_map),
        ],
        out_specs=pl.BlockSpec((1, tq, GROUP_W), lambda i, b: (b, i, 0)),
        out_shape=jax.ShapeDtypeStruct((bsz, s, GROUP_W), bf16),
        compiler_params=pltpu.CompilerParams(vmem_limit_bytes=V7X_VMEM_LIMIT),
        name="na",
    )(q, k, v, kc, vc, bias)


def _na_bias(rpb, rows):
    cidx = np.arange(GRID_W)
    col_start = np.clip(cidx - NA_WIN_COLS // 2, 0, GRID_W - NA_WIN_COLS)
    in_win = (cidx[None, :] >= col_start[:, None]) & (cidx[None, :] < col_start[:, None] + NA_WIN_COLS)
    dc = np.clip(cidx[None, :] - cidx[:, None], 1 - NA_WIN_COLS, NA_WIN_COLS - 1) + NA_WIN_COLS - 1
    t = jnp.where(in_win[None, None], rpb[:, :, dc] * LOG2E, MASKED)
    t = jnp.concatenate([t, jnp.full_like(t[:, :1], MASKED)], axis=1)
    nblk = rows // NA_Q_ROWS
    dr_idx = np.full((3, NA_Q_ROWS, NA_BAND_ROWS), 2 * NA_WIN_ROWS - 1, np.int32)
    for cat, blk in enumerate((0, 1, nblk - 1)):
        band0 = int(np.clip(blk * NA_Q_ROWS - NA_WIN_ROWS // 2, 0, rows - NA_BAND_ROWS))
        for qi in range(NA_Q_ROWS):
            r = blk * NA_Q_ROWS + qi
            r0 = int(np.clip(r - NA_WIN_ROWS // 2, 0, rows - NA_WIN_ROWS))
            for kj in range(NA_BAND_ROWS):
                kr = band0 + kj
                if r0 <= kr < r0 + NA_WIN_ROWS:
                    dr_idx[cat, qi, kj] = kr - r + NA_WIN_ROWS - 1
    big = t[:, dr_idx]
    big = big.transpose(1, 0, 2, 4, 3, 5)
    return big.reshape(3, N_HEADS, NA_Q_ROWS * GRID_W, NA_BAND_ROWS * GRID_W)


def _sgu_kernel(u_ref, v_ref, w_ref, b_ref, o_ref):
    tm = u_ref.shape[1]
    lane = lax.broadcasted_iota(jnp.int32, (SGU_CHUNK, GROUP_W), 1)
    for c in range(tm // SGU_CHUNK):
        rows = slice(c * SGU_CHUNK, (c + 1) * SGU_CHUNK)
        vchunk = v_ref[0, rows, :]
        s = b_ref[...]
        for g in range(N_HEADS):
            in_group = (lane >= g * HEAD_DIM) & (lane < (g + 1) * HEAD_DIM)
            s = jnp.where(in_group, s + _dot(w_ref[g], vchunk), s)
        o_ref[0, rows, :] = (u_ref[0, rows, :].astype(f32) * s).astype(o_ref.dtype)


def _sgu(u, v, w, b_full):
    bsz, s, _ = u.shape
    tm = min(s, 512)
    row = pl.BlockSpec((1, tm, GROUP_W), lambda b, i: (b, i, 0))
    return pl.pallas_call(
        _sgu_kernel,
        grid=(bsz, s // tm),
        in_specs=[row, row,
                  pl.BlockSpec((N_HEADS, SGU_CHUNK, SGU_CHUNK), lambda b, i: (0, 0, 0)),
                  pl.BlockSpec((SGU_CHUNK, GROUP_W), lambda b, i: (0, 0))],
        out_specs=row,
        out_shape=jax.ShapeDtypeStruct((bsz, s, GROUP_W), bf16),
        name="sgu",
    )(u, v, w, b_full)


def _attn_t_kernel(lam_ref, q_ref, k_ref, vt_ref, g_ref, o_ref, qm_ref, sa_ref, sb_ref, ma_ref, mb_ref, acc_ref,
                   *, n_maps, post_scale):
    n_streams = N_HEADS * n_maps
    width = GROUP_W // n_streams
    tq = q_ref.shape[1]
    qt = q_ref[0].astype(f32).T.astype(bf16)
    chan = lax.broadcasted_iota(jnp.int32, qt.shape, 0)
    for j in range(n_streams):
        qm_ref[j] = jnp.where((chan >= j * width) & (chan < (j + 1) * width), qt, jnp.zeros_like(qt))

    n_chunks = k_ref.shape[1] // KEY_CHUNK

    def step(j_scores, j_values, to_first):
        s_new, m_new = (sa_ref, ma_ref) if to_first else (sb_ref, mb_ref)
        s_old, m_old = (sb_ref, mb_ref) if to_first else (sa_ref, ma_ref)
        if j_scores is not None:
            mx = jnp.full((8, tq), MASKED, f32)
        if j_values is not None:
            m = m_old[...]
            acc = jnp.zeros((2 * HEAD_DIM, tq), f32)
        for c in range(n_chunks):
            keys = slice(c * KEY_CHUNK, (c + 1) * KEY_CHUNK)
            if j_scores is not None:
                s = _dot(k_ref[0, keys, :], qm_ref[j_scores])
                s_new[keys, :] = s
                mx = jnp.maximum(mx, jnp.max(s.reshape(KEY_CHUNK // 8, 8, tq), axis=0))
            if j_values is not None:
                p = jnp.exp2(s_old[keys, :] - m).astype(bf16)
                acc = acc + _dot(vt_ref[0, j_values // n_maps, :, keys], p)
        if j_scores is not None:
            m_new[...] = jnp.max(mx, axis=0, keepdims=True)
        if j_values is not None:
            acc_ref[j_values] = acc

    step(0, None, True)

    def stream_pair(i, carry):
        step(2 * i + 1, 2 * i, False)
        step(2 * i + 2, 2 * i + 1, True)
        return carry

    lax.fori_loop(0, n_streams // 2 - 1, stream_pair, 0)
    step(n_streams - 1, n_streams - 2, False)
    step(None, n_streams - 1, True)

    outs = []
    for h in range(N_HEADS):
        if n_maps == 1:
            a = acc_ref[h]
            o = a[:HEAD_DIM] / a[HEAD_DIM:HEAD_DIM + 1]
        else:
            a1 = acc_ref[2 * h]
            a2 = acc_ref[2 * h + 1]
            o = a1[:HEAD_DIM] / a1[HEAD_DIM:HEAD_DIM + 1] - lam_ref[0] * (a2[:HEAD_DIM] / a2[HEAD_DIM:HEAD_DIM + 1])
            r = lax.rsqrt(jnp.mean(o * o, axis=0, keepdims=True) + EPS)
            o = (o * r) * g_ref[...] * post_scale
        outs.append(o)
    o_ref[0] = jnp.concatenate(outs, axis=0).T.astype(o_ref.dtype)


def _attn_t(q, k, vt, lam, g_rows, n_maps, post_scale):
    bsz, s, _ = q.shape
    tq = min(s, 512)
    n_keys = k.shape[1]
    n_streams = N_HEADS * n_maps
    return pl.pallas_call(
        partial(_attn_t_kernel, n_maps=n_maps, post_scale=post_scale),
        grid=(bsz, s // tq),
        in_specs=[
            pl.BlockSpec(memory_space=pltpu.SMEM),
            pl.BlockSpec((1, tq, GROUP_W), lambda b, i: (b, i, 0)),
            pl.BlockSpec((1, n_keys, GROUP_W), lambda b, i: (b, 0, 0)),
            pl.BlockSpec((1, N_HEADS, 2 * HEAD_DIM, n_keys), lambda b, i: (b, 0, 0, 0)),
            pl.BlockSpec((HEAD_DIM, tq), lambda b, i: (0, 0)),
        ],
        out_specs=pl.BlockSpec((1, tq, GROUP_W), lambda b, i: (b, i, 0)),
        out_shape=jax.ShapeDtypeStruct((bsz, s, GROUP_W), bf16),
        scratch_shapes=[
            pltpu.VMEM((n_streams, GROUP_W, tq), bf16),
            pltpu.VMEM((n_keys, tq), f32), pltpu.VMEM((n_keys, tq), f32),
            pltpu.VMEM((1, tq), f32), pltpu.VMEM((1, tq), f32),
            pltpu.VMEM((n_streams, 2 * HEAD_DIM, tq), f32),
        ],
        compiler_params=pltpu.CompilerParams(vmem_limit_bytes=V7X_VMEM_LIMIT),
        name="attn_t%d" % n_maps,
    )(lam, q, k, vt, g_rows)


def _values_t(v):
    bsz, n, _ = v.shape
    vt = v.reshape(bsz, n, N_HEADS, HEAD_DIM).transpose(0, 2, 3, 1)
    return jnp.concatenate([vt, jnp.ones_like(vt)], axis=2)


def _post_kernel(x_ref, ya_ref, yb_ref, yc_ref, yd_ref, g1_ref, sh_ref, sc_ref, g2_ref, ng_ref, wo_ref, w1_ref,
                 w2_ref, fg_ref, o_ref, *, final):
    y = jnp.concatenate([ya_ref[0], yb_ref[0], yc_ref[0], yd_ref[0]], axis=-1)
    x = x_ref[0] + g1_ref[0] * _dot(y, wo_ref[...])
    r = lax.rsqrt(jnp.mean(x * x, axis=-1, keepdims=True) + EPS)
    h = (x * r) * ng_ref[...]
    hb = (h * (1.0 + sc_ref[0]) + sh_ref[0]).astype(bf16)
    acc = jnp.zeros(x.shape, f32)
    step = D_MODEL
    for j in range(D_FF // step):
        t = jnp.maximum(_dot(hb, w1_ref[:, j * step:(j + 1) * step]), 0.0)
        acc = acc + _dot((t * t).astype(bf16), w2_ref[j * step:(j + 1) * step, :])
    x = x + g2_ref[0] * acc
    if final:
        r = lax.rsqrt(jnp.mean(x * x, axis=-1, keepdims=True) + EPS)
        x = (x * r) * fg_ref[...]
    o_ref[0] = x


def _post(x, ys, mods, layer, ctx, ng, wo, w1, w2, fg, final):
    bsz, s, _ = x.shape
    tm = min(s, 512)
    row = lambda width: pl.BlockSpec((1, tm, width), lambda b, i: (b, i, 0))
    const = lambda shape: pl.BlockSpec(shape, lambda b, i: (0, 0), pipeline_mode=pl.Buffered(1))
    return pl.pallas_call(
        partial(_post_kernel, final=final),
        grid=(bsz, s // tm),
        in_specs=[
            row(D_MODEL), row(GROUP_W), row(GROUP_W), row(GROUP_W), row(GROUP_W),
            _mod_spec(layer, 2, ctx), _mod_spec(layer, 3, ctx), _mod_spec(layer, 4, ctx), _mod_spec(layer, 5, ctx),
            const((1, D_MODEL)), const((D_MODEL, D_MODEL)), const((D_MODEL, D_FF)), const((D_FF, D_MODEL)),
            const((1, D_MODEL)),
        ],
        out_specs=row(D_MODEL),
        out_shape=jax.ShapeDtypeStruct((bsz, s, D_MODEL), f32),
        compiler_params=pltpu.CompilerParams(vmem_limit_bytes=V7X_VMEM_LIMIT),
        name="post_ctx" if ctx else "post",
    )(x, *ys, mods, mods, mods, mods, ng, wo, w1, w2, fg)


def _channel_dft(n_pos):
    c = np.arange(HEAD_DIM)
    ang = 2.0 * np.pi * ((c[:, None] * c[None, :]) % HEAD_DIM) / HEAD_DIM
    scale = 1.0 / math.sqrt(n_pos * HEAD_DIM)
    bd = np.zeros((GROUP_W, 2 * GROUP_W), np.float32)
    for g in range(N_HEADS):
        sl = slice(g * HEAD_DIM, (g + 1) * HEAD_DIM)
        bd[sl, sl] = np.cos(ang) * scale
        bd[sl, GROUP_W + g * HEAD_DIM:GROUP_W + (g + 1) * HEAD_DIM] = -np.sin(ang) * scale
    return jnp.asarray(bd, bf16)


def _position_dft(n_pos):
    hi = max(n_pos // GRID_W, 1)
    lo = n_pos // hi
    np_ = jnp.arange(n_pos, dtype=jnp.int32)[:, None]
    a = 2.0 * np.pi * ((np_ * jnp.arange(hi, dtype=jnp.int32)[None, :] * lo) % n_pos).astype(f32) / n_pos
    b = 2.0 * np.pi * ((np_ * jnp.arange(lo, dtype=jnp.int32)[None, :]) % n_pos).astype(f32) / n_pos
    ca, sa, cb, sb = jnp.cos(a)[:, :, None], jnp.sin(a)[:, :, None], jnp.cos(b)[:, None, :], jnp.sin(b)[:, None, :]
    cn = (ca * cb - sa * sb).reshape(n_pos, n_pos)
    sn = (sa * cb + ca * sb).reshape(n_pos, n_pos)
    return cn.astype(bf16), sn.astype(bf16)


def _rope_tables(n):
    t = jnp.arange(n)
    rows = (t // GRID_W).astype(f32)
    cols = (t % GRID_W).astype(f32)
    n_freq = DIFF_QK_DIM // 4
    inv = ROPE_BASE ** (-jnp.arange(n_freq, dtype=f32) / n_freq)
    ang = jnp.concatenate([rows[:, None] * inv, cols[:, None] * inv], axis=-1)
    lane = np.arange(GROUP_W) % DIFF_QK_DIM
    src = (lane // 16) * n_freq + lane % n_freq
    sign = np.where((lane % 16) < n_freq, -1.0, 1.0).astype(np.float32)
    return jnp.cos(ang)[:, src], jnp.sin(ang)[:, src] * sign


def kernel(x, c, ctx, c_ctx, ada_w, ada_b, norm1_g, norm2_g, w_in, w_out, na_rpb, sgu_ln_g, sgu_ln_b, sgu_w, sgu_b,
           diff_lq1, diff_lk1, diff_lq2, diff_lk2, diff_subln_g, w_ff1, w_ff2, final_g):
    bsz, n, _ = x.shape
    n_ctx = ctx.shape[1]
    depth = w_in.shape[0]
    rows = n // GRID_W
    assert bsz <= COND_ROWS // 2

    cond = jnp.zeros((COND_ROWS, D_MODEL), f32).at[:bsz].set(c).at[COND_ROWS // 2].set(c_ctx)
    mods = _ada_mods(cond, ada_w, ada_b).reshape(depth * COND_ROWS * 6, 1, D_MODEL)

    cos_l, sin_l = _rope_tables(n)
    rope_lat = (cos_l * DIFF_Q_SCALE, sin_l * DIFF_Q_SCALE, cos_l, sin_l)
    ones = jnp.ones((n_ctx, GROUP_W), f32)
    rope_ctx = (ones * DIFF_Q_SCALE, ones * 0.0, ones, ones * 0.0)
    bd_lat, bd_ctx = _channel_dft(n), _channel_dft(n_ctx)
    dft_lat, dft_ctx = _position_dft(n), _position_dft(n_ctx)

    w_in_b, w_out_b = w_in.astype(bf16), w_out.astype(bf16)
    w1_b, w2_b, sgu_w_b = w_ff1.astype(bf16), w_ff2.astype(bf16), sgu_w.astype(bf16)
    fg = final_g.reshape(1, D_MODEL)
    cx = ctx

    for l in range(depth):
        last = l == depth - 1
        lam_init = 0.8 - 0.6 * math.exp(-0.3 * l)
        lam = (jnp.exp(jnp.sum(diff_lq1[l].astype(f32) * diff_lk1[l].astype(f32)))
               - jnp.exp(jnp.sum(diff_lq2[l].astype(f32) * diff_lk2[l].astype(f32))) + lam_init).reshape(1)
        g1 = norm1_g[l].reshape(1, D_MODEL)
        g2 = norm2_g[l].reshape(1, D_MODEL)
        lng = sgu_ln_g[l].reshape(1, GROUP_W)
        lnb = sgu_ln_b[l].reshape(1, GROUP_W)
        sgu_bias = jnp.repeat(sgu_b[l].T, HEAD_DIM, axis=1)
        subln = lambda width: jnp.broadcast_to(diff_subln_g[l].reshape(HEAD_DIM, 1), (HEAD_DIM, width))
        post_scale = 1.0 - lam_init

        p_lat = _proj_in(x, mods, l, False, g1, w_in_b[l], bd_lat, rope_lat, lng, lnb)
        p_ctx = _proj_in(cx, mods, l, True, g1, w_in_b[l], bd_ctx, rope_ctx, lng, lnb)
        acs, naq, nak, nav, sgu_u, sgu_v, dfq, dfk, dfv = p_lat
        c_acs, c_naq, c_nak, c_nav, c_sgu_u, c_sgu_v, c_dfq, c_dfk, c_dfv = p_ctx

        y_a = _fourier(*dft_lat, acs)
        y_b = _na(naq, nak, nav, c_nak, c_nav, _na_bias(na_rpb[l], rows))
        y_c = _sgu(sgu_u, sgu_v, sgu_w_b[l], sgu_bias)
        k_all = jnp.concatenate([dfk, c_dfk], axis=1)
        v_all = _values_t(jnp.concatenate([dfv, c_dfv], axis=1))
        y_d = _attn_t(dfq, k_all, v_all, lam, subln(min(n, 512)), 2, post_scale)

        if not last:
            yc_a = _fourier(*dft_ctx, c_acs)
            yc_b = _attn_t(c_naq, c_nak, _values_t(c_nav), lam, subln(n_ctx), 1, 1.0)
            yc_c = _sgu(c_sgu_u, c_sgu_v, sgu_w_b[l], sgu_bias)
            yc_d = _attn_t(c_dfq, c_dfk, _values_t(c_dfv), lam, subln(n_ctx), 2, post_scale)
            cx = _post(cx, (yc_a, yc_b, yc_c, yc_d), mods, l, True, g2, w_out_b[l], w1_b[l], w2_b[l], fg, False)

        x = _post(x, (y_a, y_b, y_c, y_d), mods, l, False, g2, w_out_b[l], w1_b[l], w2_b[l], fg, last)
    return x
```

```python
import math
from functools import partial

import numpy as np
import jax
import jax.numpy as jnp
from jax import lax
from jax.experimental import pallas as pl
from jax.experimental.pallas import tpu as pltpu

D_MODEL = 1024
GRID_W = 64
N_HEADS = 4
GROUP_W = 256
HEAD_DIM = 64
NA_WIN_ROWS = 8
NA_WIN_COLS = 16
SGU_CHUNK = 128
DIFF_QK_DIM = 32
ROPE_BASE = 10000.0
D_FF = 4 * D_MODEL
D_IN = 9 * GROUP_W
EPS = 1e-6

LOG2E = 1.4426950408889634
NA_Q_SCALE = HEAD_DIM ** -0.5 * LOG2E
DIFF_Q_SCALE = DIFF_QK_DIM ** -0.5 * LOG2E
MASKED = -1e30
COND_ROWS = 16
KEY_CHUNK = 256
NA_Q_ROWS = 8
NA_BAND_ROWS = 16
V7X_VMEM_LIMIT = 56 * 1024 * 1024

f32 = jnp.float32
bf16 = jnp.bfloat16


def _dot(a, b):
    return jnp.dot(a, b, preferred_element_type=f32)


def _ada_kernel(cond_ref, w_ref, b_ref, o_ref):
    c = cond_ref[...]
    s = (c * jax.nn.sigmoid(c)).astype(bf16)
    o_ref[0] = _dot(s, w_ref[0].astype(bf16)) + b_ref[0]


def _ada_mods(cond, ada_w, ada_b):
    depth = ada_w.shape[0]
    tn = 1536
    return pl.pallas_call(
        _ada_kernel,
        grid=(depth, 6 * D_MODEL // tn),
        in_specs=[
            pl.BlockSpec((COND_ROWS, D_MODEL), lambda l, j: (0, 0)),
            pl.BlockSpec((1, D_MODEL, tn), lambda l, j: (l, 0, j)),
            pl.BlockSpec((1, 1, tn), lambda l, j: (l, 0, j)),
        ],
        out_specs=pl.BlockSpec((1, COND_ROWS, tn), lambda l, j: (l, 0, j)),
        out_shape=jax.ShapeDtypeStruct((depth, COND_ROWS, 6 * D_MODEL), f32),
        compiler_params=pltpu.CompilerParams(vmem_limit_bytes=V7X_VMEM_LIMIT),
        name="ada",
    )(cond, ada_w, ada_b.reshape(depth, 1, 6 * D_MODEL))


def _mod_spec(layer, which, ctx):
    if ctx:
        return pl.BlockSpec((1, 1, D_MODEL), lambda b, i: ((layer * COND_ROWS + COND_ROWS // 2) * 6 + which, 0, 0))
    return pl.BlockSpec((1, 1, D_MODEL), lambda b, i: ((layer * COND_ROWS + b) * 6 + which, 0, 0))


def _rope_partner(x):
    lane = lax.broadcasted_iota(jnp.int32, x.shape, 1)
    first_half = (lane % 16) < 8
    n = x.shape[1]
    return jnp.where(first_half, pltpu.roll(x, n - 8, 1), pltpu.roll(x, 8, 1))


def _values_t(v):
    vt = v.T.astype(bf16).reshape(N_HEADS, HEAD_DIM, v.shape[0])
    return jnp.concatenate([vt, jnp.ones_like(vt)], axis=1)


def _proj_in_kernel(x_ref, sh_ref, sc_ref, g_ref, w_ref, bd_ref, cq_ref, sq_ref, ck_ref, sk_ref, lng_ref, lnb_ref,
                    acs_ref, naq_ref, nak_ref, nav_ref, sgu_ref, sgv_ref, dfq_ref, dfk_ref, dfv_ref):
    x = x_ref[0]
    r = lax.rsqrt(jnp.mean(x * x, axis=-1, keepdims=True) + EPS)
    h = (x * r) * g_ref[...]
    h = h * (1.0 + sc_ref[0]) + sh_ref[0]
    hb = h.astype(bf16)
    gw = GROUP_W

    def mm(col, width=gw):
        return _dot(hb, w_ref[:, col:col + width])

    acs_ref[0] = _dot(mm(0).astype(bf16), bd_ref[...]).astype(bf16)
    naq_ref[0] = (mm(gw) * NA_Q_SCALE).astype(bf16)
    nak_ref[0] = mm(2 * gw).astype(bf16)
    nav_ref[0] = _values_t(mm(3 * gw))
    uv = jax.nn.gelu(mm(4 * gw, 2 * gw), approximate=True)
    sgu_ref[0] = uv[:, :gw].astype(bf16)
    v = uv[:, gw:]
    mu = jnp.mean(v, axis=-1, keepdims=True)
    vc = v - mu
    var = jnp.mean(vc * vc, axis=-1, keepdims=True)
    sgv_ref[0] = (vc * lax.rsqrt(var + EPS) * lng_ref[...] + lnb_ref[...]).astype(bf16)
    q = mm(6 * gw)
    dfq_ref[0] = (q * cq_ref[...] + _rope_partner(q) * sq_ref[...]).astype(bf16)
    k = mm(7 * gw)
    dfk_ref[0] = (k * ck_ref[...] + _rope_partner(k) * sk_ref[...]).astype(bf16)
    dfv_ref[0] = _values_t(mm(8 * gw))


def _proj_in(x, mods, layer, ctx, g, w, bd, rope, lng, lnb):
    bsz, s, _ = x.shape
    tm = min(s, 512)
    row = lambda b, i: (b, i, 0)
    const = lambda b, i: (0, 0)
    pos = lambda b, i: (i, 0)
    out = lambda width: pl.BlockSpec((1, tm, width), row)
    shape = lambda width: jax.ShapeDtypeStruct((bsz, s, width), bf16)
    out_vt = pl.BlockSpec((1, N_HEADS, 2 * HEAD_DIM, tm), lambda b, i: (b, 0, 0, i))
    shape_vt = jax.ShapeDtypeStruct((bsz, N_HEADS, 2 * HEAD_DIM, s), bf16)
    widths = (2 * GROUP_W, GROUP_W, GROUP_W, None, GROUP_W, GROUP_W, GROUP_W, GROUP_W, None)
    return pl.pallas_call(
        _proj_in_kernel,
        grid=(bsz, s // tm),
        in_specs=[
            pl.BlockSpec((1, tm, D_MODEL), row),
            _mod_spec(layer, 0, ctx), _mod_spec(layer, 1, ctx),
            pl.BlockSpec((1, D_MODEL), const),
            pl.BlockSpec((D_MODEL, D_IN), const),
            pl.BlockSpec((GROUP_W, 2 * GROUP_W), const),
            pl.BlockSpec((tm, GROUP_W), pos), pl.BlockSpec((tm, GROUP_W), pos),
            pl.BlockSpec((tm, GROUP_W), pos), pl.BlockSpec((tm, GROUP_W), pos),
            pl.BlockSpec((1, GROUP_W), const), pl.BlockSpec((1, GROUP_W), const),
        ],
        out_specs=[out_vt if wd is None else out(wd) for wd in widths],
        out_shape=[shape_vt if wd is None else shape(wd) for wd in widths],
        compiler_params=pltpu.CompilerParams(vmem_limit_bytes=V7X_VMEM_LIMIT),
        name="proj_in_ctx" if ctx else "proj_in",
    )(x, mods, mods, g, w, bd, *rope, lng, lnb)


def _fourier_kernel(c_ref, s_ref, a_ref, o_ref, acc_ref):
    k = pl.program_id(1)
    gw = GROUP_W

    @pl.when(k == 0)
    def _():
        acc_ref[...] = jnp.zeros_like(acc_ref)

    for b in range(a_ref.shape[0]):
        acc_ref[b] += _dot(c_ref[...], a_ref[b, :, :gw]) + _dot(s_ref[...], a_ref[b, :, gw:])

    @pl.when(k == pl.num_programs(1) - 1)
    def _():
        o_ref[...] = acc_ref[...].astype(o_ref.dtype)


def _fourier(cn, sn, acs):
    bsz, s, _ = acs.shape
    tm = min(s, 1024)
    tk = min(s, 512)
    return pl.pallas_call(
        _fourier_kernel,
        grid=(s // tm, s // tk),
        in_specs=[
            pl.BlockSpec((tm, tk), lambda i, k: (i, k)),
            pl.BlockSpec((tm, tk), lambda i, k: (i, k)),
            pl.BlockSpec((bsz, tk, 2 * GROUP_W), lambda i, k: (0, k, 0)),
        ],
        out_specs=pl.BlockSpec((bsz, tm, GROUP_W), lambda i, k: (0, i, 0)),
        out_shape=jax.ShapeDtypeStruct((bsz, s, GROUP_W), bf16),
        scratch_shapes=[pltpu.VMEM((bsz, tm, GROUP_W), f32)],
        compiler_params=pltpu.CompilerParams(vmem_limit_bytes=V7X_VMEM_LIMIT),
        name="fourier",
    )(cn, sn, acs)


def _na_band_start(i):
    rows = GRID_W
    return jnp.clip(i * NA_Q_ROWS - NA_WIN_ROWS // 2, 0, rows - NA_BAND_ROWS)


def _na_bias_t(rpb, rows):
    cidx = np.arange(GRID_W)
    col_start = np.clip(cidx - NA_WIN_COLS // 2, 0, GRID_W - NA_WIN_COLS)
    in_win = (cidx[None, :] >= col_start[:, None]) & (cidx[None, :] < col_start[:, None] + NA_WIN_COLS)
    dc = np.clip(cidx[None, :] - cidx[:, None], 1 - NA_WIN_COLS, NA_WIN_COLS - 1) + NA_WIN_COLS - 1
    t = jnp.where(in_win.T[None, None], rpb[:, :, dc.T] * LOG2E, MASKED)
    outside = jnp.full_like(t[:, 0], MASKED)
    nblk = rows // NA_Q_ROWS
    cats = []
    for blk in (0, 1, nblk - 1):
        band0 = int(np.clip(blk * NA_Q_ROWS - NA_WIN_ROWS // 2, 0, rows - NA_BAND_ROWS))
        key_rows = []
        for kj in range(NA_BAND_ROWS):
            kr = band0 + kj
            per_q = []
            for qi in range(NA_Q_ROWS):
                r = blk * NA_Q_ROWS + qi
                r0 = int(np.clip(r - NA_WIN_ROWS // 2, 0, rows - NA_WIN_ROWS))
                per_q.append(t[:, kr - r + NA_WIN_ROWS - 1] if r0 <= kr < r0 + NA_WIN_ROWS else outside)
            key_rows.append(jnp.concatenate(per_q, axis=-1))
        cats.append(jnp.concatenate(key_rows, axis=1))
    return jnp.stack(cats)


def _sgu_kernel(u_ref, v_ref, w_ref, b_ref, o_ref):
    tm = u_ref.shape[1]
    lane = lax.broadcasted_iota(jnp.int32, (SGU_CHUNK, GROUP_W), 1)
    for c in range(tm // SGU_CHUNK):
        rows = slice(c * SGU_CHUNK, (c + 1) * SGU_CHUNK)
        vchunk = v_ref[0, rows, :]
        s = b_ref[...]
        for g in range(N_HEADS):
            in_group = (lane >= g * HEAD_DIM) & (lane < (g + 1) * HEAD_DIM)
            s = jnp.where(in_group, s + _dot(w_ref[g], vchunk), s)
        o_ref[0, rows, :] = (u_ref[0, rows, :].astype(f32) * s).astype(o_ref.dtype)


def _sgu(u, v, w, b_full):
    bsz, s, _ = u.shape
    tm = min(s, 512)
    row = pl.BlockSpec((1, tm, GROUP_W), lambda b, i: (b, i, 0))
    return pl.pallas_call(
        _sgu_kernel,
        grid=(bsz, s // tm),
        in_specs=[row, row,
                  pl.BlockSpec((N_HEADS, SGU_CHUNK, SGU_CHUNK), lambda b, i: (0, 0, 0)),
                  pl.BlockSpec((SGU_CHUNK, GROUP_W), lambda b, i: (0, 0))],
        out_specs=row,
        out_shape=jax.ShapeDtypeStruct((bsz, s, GROUP_W), bf16),
        name="sgu",
    )(u, v, w, b_full)


def _attn_t_kernel(*refs, n_maps, post_scale, sources):
    lam_ref, q_ref = refs[:2]
    pos = 2
    band0 = pl.multiple_of(_na_band_start(pl.program_id(1)) * GRID_W, KEY_CHUNK)
    chunks = []
    for n_keys, banded, biased in sources:
        k_ref, vt_ref = refs[pos:pos + 2]
        bias_ref = refs[pos + 2] if biased else None
        pos += 3 if biased else 2
        for off in range(0, n_keys, KEY_CHUNK):
            chunks.append((k_ref, vt_ref, bias_ref, pl.multiple_of(band0 + off, KEY_CHUNK) if banded else off, off))
    g_ref, o_ref, qm_ref, sa_ref, sb_ref, ma_ref, mb_ref, acc_ref = refs[pos:]

    n_streams = N_HEADS * n_maps
    width = GROUP_W // n_streams
    tq = q_ref.shape[1]
    qt = q_ref[0].astype(f32).T.astype(bf16)
    chan = lax.broadcasted_iota(jnp.int32, qt.shape, 0)
    for j in range(n_streams):
        qm_ref[j] = jnp.where((chan >= j * width) & (chan < (j + 1) * width), qt, jnp.zeros_like(qt))

    def step(j_scores, j_values, to_first):
        s_new, m_new = (sa_ref, ma_ref) if to_first else (sb_ref, mb_ref)
        s_old, m_old = (sb_ref, mb_ref) if to_first else (sa_ref, ma_ref)
        if j_scores is not None:
            mx = jnp.full((8, tq), MASKED, f32)
        if j_values is not None:
            m = m_old[...]
            acc = jnp.zeros((2 * HEAD_DIM, tq), f32)
        for c, (k_ref, vt_ref, bias_ref, first, bias_first) in enumerate(chunks):
            rows = slice(c * KEY_CHUNK, (c + 1) * KEY_CHUNK)
            if j_scores is not None:
                s = _dot(k_ref[0, pl.ds(first, KEY_CHUNK), :], qm_ref[j_scores])
                if bias_ref is not None:
                    s = s + bias_ref[0, j_scores // n_maps, bias_first:bias_first + KEY_CHUNK, :]
                s_new[rows, :] = s
                mx = jnp.maximum(mx, jnp.max(s.reshape(KEY_CHUNK // 8, 8, tq), axis=0))
            if j_values is not None:
                p = jnp.exp2(s_old[rows, :] - m).astype(bf16)
                acc = acc + _dot(vt_ref[0, j_values // n_maps, :, pl.ds(first, KEY_CHUNK)], p)
        if j_scores is not None:
            m_new[...] = jnp.max(mx, axis=0, keepdims=True)
        if j_values is not None:
            acc_ref[j_values] = acc

    step(0, None, True)

    def stream_pair(i, carry):
        step(2 * i + 1, 2 * i, False)
        step(2 * i + 2, 2 * i + 1, True)
        return carry

    lax.fori_loop(0, n_streams // 2 - 1, stream_pair, 0)
    step(n_streams - 1, n_streams - 2, False)
    step(None, n_streams - 1, True)

    outs = []
    for h in range(N_HEADS):
        if n_maps == 1:
            a = acc_ref[h]
            o = a[:HEAD_DIM] / a[HEAD_DIM:HEAD_DIM + 1]
        else:
            a1 = acc_ref[2 * h]
            a2 = acc_ref[2 * h + 1]
            o = a1[:HEAD_DIM] / a1[HEAD_DIM:HEAD_DIM + 1] - lam_ref[0] * (a2[:HEAD_DIM] / a2[HEAD_DIM:HEAD_DIM + 1])
            r = lax.rsqrt(jnp.mean(o * o, axis=0, keepdims=True) + EPS)
            o = (o * r) * g_ref[...] * post_scale
        outs.append(o)
    o_ref[0] = jnp.concatenate(outs, axis=0).T.astype(o_ref.dtype)


def _attn_t(name, q, srcs, lam, g_rows, n_maps, post_scale):
    bsz, s, _ = q.shape
    tq = min(s, 512)
    nblk = s // tq
    n_streams = N_HEADS * n_maps
    n_total = sum(src[3] for src in srcs)
    in_specs = [
        pl.BlockSpec(memory_space=pltpu.SMEM),
        pl.BlockSpec((1, tq, GROUP_W), lambda b, i: (b, i, 0)),
    ]
    args = [lam, q]
    for k, vt, bias, n_used, banded in srcs:
        in_specs += [
            pl.BlockSpec((1, k.shape[1], GROUP_W), lambda b, i: (b, 0, 0)),
            pl.BlockSpec((1, N_HEADS, 2 * HEAD_DIM, k.shape[1]), lambda b, i: (b, 0, 0, 0)),
        ]
        args += [k, vt]
        if bias is not None:
            in_specs.append(pl.BlockSpec(
                (1, N_HEADS, n_used, tq), lambda b, i: (jnp.where(i == 0, 0, jnp.where(i == nblk - 1, 2, 1)), 0, 0, 0)))
            args.append(bias)
    in_specs.append(pl.BlockSpec((HEAD_DIM, tq), lambda b, i: (0, 0)))
    args.append(g_rows)
    return pl.pallas_call(
        partial(_attn_t_kernel, n_maps=n_maps, post_scale=post_scale,
                sources=tuple((src[3], src[4], src[2] is not None) for src in srcs)),
        grid=(bsz, nblk),
        in_specs=in_specs,
        out_specs=pl.BlockSpec((1, tq, GROUP_W), lambda b, i: (b, i, 0)),
        out_shape=jax.ShapeDtypeStruct((bsz, s, GROUP_W), bf16),
        scratch_shapes=[
            pltpu.VMEM((n_streams, GROUP_W, tq), bf16),
            pltpu.VMEM((n_total, tq), f32), pltpu.VMEM((n_total, tq), f32),
            pltpu.VMEM((1, tq), f32), pltpu.VMEM((1, tq), f32),
            pltpu.VMEM((n_streams, 2 * HEAD_DIM, tq), f32),
        ],
        compiler_params=pltpu.CompilerParams(vmem_limit_bytes=V7X_VMEM_LIMIT),
        name=name,
    )(*args)


def _post_kernel(x_ref, ya_ref, yb_ref, yc_ref, yd_ref, g1_ref, sh_ref, sc_ref, g2_ref, ng_ref, wo_ref, w1_ref,
                 w2_ref, fg_ref, o_ref, *, final):
    y = jnp.concatenate([ya_ref[0], yb_ref[0], yc_ref[0], yd_ref[0]], axis=-1)
    x = x_ref[0] + g1_ref[0] * _dot(y, wo_ref[...])
    r = lax.rsqrt(jnp.mean(x * x, axis=-1, keepdims=True) + EPS)
    h = (x * r) * ng_ref[...]
    hb = (h * (1.0 + sc_ref[0]) + sh_ref[0]).astype(bf16)
    acc = jnp.zeros(x.shape, f32)
    step = D_MODEL
    for j in range(D_FF // step):
        t = jnp.maximum(_dot(hb, w1_ref[:, j * step:(j + 1) * step]), 0.0)
        acc = acc + _dot((t * t).astype(bf16), w2_ref[j * step:(j + 1) * step, :])
    x = x + g2_ref[0] * acc
    if final:
        r = lax.rsqrt(jnp.mean(x * x, axis=-1, keepdims=True) + EPS)
        x = (x * r) * fg_ref[...]
    o_ref[0] = x


def _post(x, ys, mods, layer, ctx, ng, wo, w1, w2, fg, final):
    bsz, s, _ = x.shape
    tm = min(s, 512)
    row = lambda width: pl.BlockSpec((1, tm, width), lambda b, i: (b, i, 0))
    const = lambda shape: pl.BlockSpec(shape, lambda b, i: (0, 0), pipeline_mode=pl.Buffered(1))
    return pl.pallas_call(
        partial(_post_kernel, final=final),
        grid=(bsz, s // tm),
        in_specs=[
            row(D_MODEL), row(GROUP_W), row(GROUP_W), row(GROUP_W), row(GROUP_W),
            _mod_spec(layer, 2, ctx), _mod_spec(layer, 3, ctx), _mod_spec(layer, 4, ctx), _mod_spec(layer, 5, ctx),
            const((1, D_MODEL)), const((D_MODEL, D_MODEL)), const((D_MODEL, D_FF)), const((D_FF, D_MODEL)),
            const((1, D_MODEL)),
        ],
        out_specs=row(D_MODEL),
        out_shape=jax.ShapeDtypeStruct((bsz, s, D_MODEL), f32),
        compiler_params=pltpu.CompilerParams(vmem_limit_bytes=V7X_VMEM_LIMIT),
        name="post_ctx" if ctx else "post",
    )(x, *ys, mods, mods, mods, mods, ng, wo, w1, w2, fg)


def _channel_dft(n_pos):
    c = np.arange(HEAD_DIM)
    ang = 2.0 * np.pi * ((c[:, None] * c[None, :]) % HEAD_DIM) / HEAD_DIM
    scale = 1.0 / math.sqrt(n_pos * HEAD_DIM)
    bd = np.zeros((GROUP_W, 2 * GROUP_W), np.float32)
    for g in range(N_HEADS):
        sl = slice(g * HEAD_DIM, (g + 1) * HEAD_DIM)
        bd[sl, sl] = np.cos(ang) * scale
        bd[sl, GROUP_W + g * HEAD_DIM:GROUP_W + (g + 1) * HEAD_DIM] = -np.sin(ang) * scale
    return jnp.asarray(bd, bf16)


def _position_dft(n_pos):
    hi = max(n_pos // GRID_W, 1)
    lo = n_pos // hi
    np_ = jnp.arange(n_pos, dtype=jnp.int32)[:, None]
    a = 2.0 * np.pi * ((np_ * jnp.arange(hi, dtype=jnp.int32)[None, :] * lo) % n_pos).astype(f32) / n_pos
    b = 2.0 * np.pi * ((np_ * jnp.arange(lo, dtype=jnp.int32)[None, :]) % n_pos).astype(f32) / n_pos
    ca, sa, cb, sb = jnp.cos(a)[:, :, None], jnp.sin(a)[:, :, None], jnp.cos(b)[:, None, :], jnp.sin(b)[:, None, :]
    cn = (ca * cb - sa * sb).reshape(n_pos, n_pos)
    sn = (sa * cb + ca * sb).reshape(n_pos, n_pos)
    return cn.astype(bf16), sn.astype(bf16)


def _rope_tables(n):
    t = jnp.arange(n)
    rows = (t // GRID_W).astype(f32)
    cols = (t % GRID_W).astype(f32)
    n_freq = DIFF_QK_DIM // 4
    inv = ROPE_BASE ** (-jnp.arange(n_freq, dtype=f32) / n_freq)
    ang = jnp.concatenate([rows[:, None] * inv, cols[:, None] * inv], axis=-1)
    lane = np.arange(GROUP_W) % DIFF_QK_DIM
    src = (lane // 16) * n_freq + lane % n_freq
    sign = np.where((lane % 16) < n_freq, -1.0, 1.0).astype(np.float32)
    return jnp.cos(ang)[:, src], jnp.sin(ang)[:, src] * sign


def kernel(x, c, ctx, c_ctx, ada_w, ada_b, norm1_g, norm2_g, w_in, w_out, na_rpb, sgu_ln_g, sgu_ln_b, sgu_w, sgu_b,
           diff_lq1, diff_lk1, diff_lq2, diff_lk2, diff_subln_g, w_ff1, w_ff2, final_g):
    bsz, n, _ = x.shape
    n_ctx = ctx.shape[1]
    depth = w_in.shape[0]
    rows = n // GRID_W
    assert bsz <= COND_ROWS // 2

    cond = jnp.zeros((COND_ROWS, D_MODEL), f32).at[:bsz].set(c).at[COND_ROWS // 2].set(c_ctx)
    mods = _ada_mods(cond, ada_w, ada_b).reshape(depth * COND_ROWS * 6, 1, D_MODEL)

    cos_l, sin_l = _rope_tables(n)
    rope_lat = (cos_l * DIFF_Q_SCALE, sin_l * DIFF_Q_SCALE, cos_l, sin_l)
    ones = jnp.ones((n_ctx, GROUP_W), f32)
    rope_ctx = (ones * DIFF_Q_SCALE, ones * 0.0, ones, ones * 0.0)
    bd_lat, bd_ctx = _channel_dft(n), _channel_dft(n_ctx)
    dft_lat, dft_ctx = _position_dft(n), _position_dft(n_ctx)

    w_in_b, w_out_b = w_in.astype(bf16), w_out.astype(bf16)
    w1_b, w2_b, sgu_w_b = w_ff1.astype(bf16), w_ff2.astype(bf16), sgu_w.astype(bf16)
    fg = final_g.reshape(1, D_MODEL)
    cx = ctx

    for l in range(depth):
        last = l == depth - 1
        lam_init = 0.8 - 0.6 * math.exp(-0.3 * l)
        lam = (jnp.exp(jnp.sum(diff_lq1[l].astype(f32) * diff_lk1[l].astype(f32)))
               - jnp.exp(jnp.sum(diff_lq2[l].astype(f32) * diff_lk2[l].astype(f32))) + lam_init).reshape(1)
        g1 = norm1_g[l].reshape(1, D_MODEL)
        g2 = norm2_g[l].reshape(1, D_MODEL)
        lng = sgu_ln_g[l].reshape(1, GROUP_W)
        lnb = sgu_ln_b[l].reshape(1, GROUP_W)
        sgu_bias = jnp.repeat(sgu_b[l].T, HEAD_DIM, axis=1)
        subln = lambda width: jnp.broadcast_to(diff_subln_g[l].reshape(HEAD_DIM, 1), (HEAD_DIM, width))
        post_scale = 1.0 - lam_init

        p_lat = _proj_in(x, mods, l, False, g1, w_in_b[l], bd_lat, rope_lat, lng, lnb)
        p_ctx = _proj_in(cx, mods, l, True, g1, w_in_b[l], bd_ctx, rope_ctx, lng, lnb)
        acs, naq, nak, nav, sgu_u, sgu_v, dfq, dfk, dfv = p_lat
        c_acs, c_naq, c_nak, c_nav, c_sgu_u, c_sgu_v, c_dfq, c_dfk, c_dfv = p_ctx

        y_a = _fourier(*dft_lat, acs)
        ctx_na = (c_nak, c_nav, None, n_ctx, False)
        ctx_df = (c_dfk, c_dfv, None, n_ctx, False)
        band = NA_BAND_ROWS * GRID_W
        y_b = _attn_t("na", naq, [(nak, nav, _na_bias_t(na_rpb[l], rows), band, True), ctx_na],
                      lam, subln(min(n, 512)), 1, 1.0)
        y_c = _sgu(sgu_u, sgu_v, sgu_w_b[l], sgu_bias)
        y_d = _attn_t("diff", dfq, [(dfk, dfv, None, n, False), ctx_df], lam, subln(min(n, 512)), 2, post_scale)

        if not last:
            yc_a = _fourier(*dft_ctx, c_acs)
            yc_b = _attn_t("dense_ctx", c_naq, [ctx_na], lam, subln(n_ctx), 1, 1.0)
            yc_c = _sgu(c_sgu_u, c_sgu_v, sgu_w_b[l], sgu_bias)
            yc_d = _attn_t("diff_ctx", c_dfq, [ctx_df], lam, subln(n_ctx), 2, post_scale)
            cx = _post(cx, (yc_a, yc_b, yc_c, yc_d), mods, l, True, g2, w_out_b[l], w1_b[l], w2_b[l], fg, False)

        x = _post(x, (y_a, y_b, y_c, y_d), mods, l, False, g2, w_out_b[l], w1_b[l], w2_b[l], fg, last)
    return x
```

```python
import math
from functools import partial

import numpy as np
import jax
import jax.numpy as jnp
from jax import lax
from jax.experimental import pallas as pl
from jax.experimental.pallas import tpu as pltpu

D_MODEL = 1024
GRID_W = 64
N_HEADS = 4
GROUP_W = 256
HEAD_DIM = 64
NA_WIN_ROWS = 8
NA_WIN_COLS = 16
SGU_CHUNK = 128
DIFF_QK_DIM = 32
ROPE_BASE = 10000.0
D_FF = 4 * D_MODEL
D_IN = 9 * GROUP_W
EPS = 1e-6

LOG2E = 1.4426950408889634
NA_Q_SCALE = HEAD_DIM ** -0.5 * LOG2E
DIFF_Q_SCALE = DIFF_QK_DIM ** -0.5 * LOG2E
MASKED = -1e30
COND_ROWS = 16
KEY_CHUNK = 256
NA_Q_ROWS = 8
NA_BAND_ROWS = 16
PROJ_ROWS = 1024
PROJ_SUB_ROWS = 256
V7X_VMEM_LIMIT = 56 * 1024 * 1024

f32 = jnp.float32
bf16 = jnp.bfloat16


def _dot(a, b):
    return jnp.dot(a, b, preferred_element_type=f32)


def _ada_kernel(cond_ref, w_ref, b_ref, o_ref):
    c = cond_ref[...]
    s = (c * jax.nn.sigmoid(c)).astype(bf16)
    o_ref[0] = _dot(s, w_ref[0].astype(bf16)) + b_ref[0]


def _ada_mods(cond, ada_w, ada_b):
    depth = ada_w.shape[0]
    tn = 1536
    return pl.pallas_call(
        _ada_kernel,
        grid=(depth, 6 * D_MODEL // tn),
        in_specs=[
            pl.BlockSpec((COND_ROWS, D_MODEL), lambda l, j: (0, 0)),
            pl.BlockSpec((1, D_MODEL, tn), lambda l, j: (l, 0, j)),
            pl.BlockSpec((1, 1, tn), lambda l, j: (l, 0, j)),
        ],
        out_specs=pl.BlockSpec((1, COND_ROWS, tn), lambda l, j: (l, 0, j)),
        out_shape=jax.ShapeDtypeStruct((depth, COND_ROWS, 6 * D_MODEL), f32),
        compiler_params=pltpu.CompilerParams(vmem_limit_bytes=V7X_VMEM_LIMIT),
        name="ada",
    )(cond, ada_w, ada_b.reshape(depth, 1, 6 * D_MODEL))


def _mod_spec(layer, which, ctx):
    if ctx:
        return pl.BlockSpec((1, 1, D_MODEL), lambda b, i: ((layer * COND_ROWS + COND_ROWS // 2) * 6 + which, 0, 0))
    return pl.BlockSpec((1, 1, D_MODEL), lambda b, i: ((layer * COND_ROWS + b) * 6 + which, 0, 0))


def _rope_partner(x):
    lane = lax.broadcasted_iota(jnp.int32, x.shape, 1)
    first_half = (lane % 16) < 8
    n = x.shape[1]
    return jnp.where(first_half, pltpu.roll(x, n - 8, 1), pltpu.roll(x, 8, 1))


def _values_t(v):
    vt = v.T.astype(bf16).reshape(N_HEADS, HEAD_DIM, v.shape[0])
    return jnp.concatenate([vt, jnp.ones_like(vt)], axis=1)


def _proj_in_kernel(x_ref, sh_ref, sc_ref, g_ref, w_ref, bd_ref, cq_ref, sq_ref, ck_ref, sk_ref, lng_ref, lnb_ref,
                    acs_ref, naq_ref, nak_ref, nav_ref, sgu_ref, sgv_ref, dfq_ref, dfk_ref, dfv_ref):
    gw = GROUP_W
    gain = g_ref[...] * (1.0 + sc_ref[0])
    sub = min(x_ref.shape[1], PROJ_SUB_ROWS)
    for r0 in range(0, x_ref.shape[1], sub):
        rows = slice(r0, r0 + sub)
        x = x_ref[0, rows, :]
        r = lax.rsqrt(jnp.mean(x * x, axis=-1, keepdims=True) + EPS)
        hb = ((x * r) * gain + sh_ref[0]).astype(bf16)

        def mm(col, width=gw):
            return _dot(hb, w_ref[:, col:col + width])

        acs_ref[0, rows, :] = _dot(mm(0).astype(bf16), bd_ref[...]).astype(bf16)
        q = mm(6 * gw)
        dfq_ref[0, rows, :] = (q * cq_ref[rows, :] + _rope_partner(q) * sq_ref[rows, :]).astype(bf16)
        k = mm(7 * gw)
        dfk_ref[0, rows, :] = (k * ck_ref[rows, :] + _rope_partner(k) * sk_ref[rows, :]).astype(bf16)
        dfv_ref[0, :, :, rows] = _values_t(mm(8 * gw))
        uv = jax.nn.gelu(mm(4 * gw, 2 * gw), approximate=True)
        sgu_ref[0, rows, :] = uv[:, :gw].astype(bf16)
        v = uv[:, gw:]
        mu = jnp.mean(v, axis=-1, keepdims=True)
        vc = v - mu
        var = jnp.mean(vc * vc, axis=-1, keepdims=True)
        sgv_ref[0, rows, :] = (vc * lax.rsqrt(var + EPS) * lng_ref[...] + lnb_ref[...]).astype(bf16)
        nav_ref[0, :, :, rows] = _values_t(mm(3 * gw))
        naq_ref[0, rows, :] = (mm(gw) * NA_Q_SCALE).astype(bf16)
        nak_ref[0, rows, :] = mm(2 * gw).astype(bf16)


def _proj_in(x, mods, layer, ctx, g, w, bd, rope, lng, lnb):
    bsz, s, _ = x.shape
    tm = min(s, PROJ_ROWS)
    row = lambda b, i: (b, i, 0)
    const = lambda b, i: (0, 0)
    of_layer = lambda b, i: (layer, 0, 0)
    pos = lambda b, i: (i, 0)
    out = lambda width: pl.BlockSpec((1, tm, width), row)
    shape = lambda width: jax.ShapeDtypeStruct((bsz, s, width), bf16)
    out_vt = pl.BlockSpec((1, N_HEADS, 2 * HEAD_DIM, tm), lambda b, i: (b, 0, 0, i))
    shape_vt = jax.ShapeDtypeStruct((bsz, N_HEADS, 2 * HEAD_DIM, s), bf16)
    widths = (2 * GROUP_W, GROUP_W, GROUP_W, None, GROUP_W, GROUP_W, GROUP_W, GROUP_W, None)
    return pl.pallas_call(
        _proj_in_kernel,
        grid=(bsz, s // tm),
        in_specs=[
            pl.BlockSpec((1, tm, D_MODEL), row),
            _mod_spec(layer, 0, ctx), _mod_spec(layer, 1, ctx),
            pl.BlockSpec((None, 1, D_MODEL), of_layer),
            pl.BlockSpec((None, D_MODEL, D_IN), of_layer),
            pl.BlockSpec((GROUP_W, 2 * GROUP_W), const),
            pl.BlockSpec((tm, GROUP_W), pos), pl.BlockSpec((tm, GROUP_W), pos),
            pl.BlockSpec((tm, GROUP_W), pos), pl.BlockSpec((tm, GROUP_W), pos),
            pl.BlockSpec((None, 1, GROUP_W), of_layer), pl.BlockSpec((None, 1, GROUP_W), of_layer),
        ],
        out_specs=[out_vt if wd is None else out(wd) for wd in widths],
        out_shape=[shape_vt if wd is None else shape(wd) for wd in widths],
        compiler_params=pltpu.CompilerParams(vmem_limit_bytes=V7X_VMEM_LIMIT),
        name="proj_in_ctx" if ctx else "proj_in",
    )(x, mods, mods, g, w, bd, *rope, lng, lnb)


def _fourier_kernel(c_ref, s_ref, a_ref, o_ref, acc_ref):
    k = pl.program_id(1)
    gw = GROUP_W

    @pl.when(k == 0)
    def _():
        acc_ref[...] = jnp.zeros_like(acc_ref)

    for b in range(a_ref.shape[0]):
        acc_ref[b] += _dot(c_ref[...], a_ref[b, :, :gw]) + _dot(s_ref[...], a_ref[b, :, gw:])

    @pl.when(k == pl.num_programs(1) - 1)
    def _():
        o_ref[...] = acc_ref[...].astype(o_ref.dtype)


def _fourier(cn, sn, acs):
    bsz, s, _ = acs.shape
    tm = min(s, 1024)
    tk = min(s, 512)
    return pl.pallas_call(
        _fourier_kernel,
        grid=(s // tm, s // tk),
        in_specs=[
            pl.BlockSpec((tm, tk), lambda i, k: (i, k)),
            pl.BlockSpec((tm, tk), lambda i, k: (i, k)),
            pl.BlockSpec((bsz, tk, 2 * GROUP_W), lambda i, k: (0, k, 0)),
        ],
        out_specs=pl.BlockSpec((bsz, tm, GROUP_W), lambda i, k: (0, i, 0)),
        out_shape=jax.ShapeDtypeStruct((bsz, s, GROUP_W), bf16),
        scratch_shapes=[pltpu.VMEM((bsz, tm, GROUP_W), f32)],
        compiler_params=pltpu.CompilerParams(vmem_limit_bytes=V7X_VMEM_LIMIT),
        name="fourier",
    )(cn, sn, acs)


def _na_band_start(i):
    rows = GRID_W
    return jnp.clip(i * NA_Q_ROWS - NA_WIN_ROWS // 2, 0, rows - NA_BAND_ROWS)


def _na_bias_t(rpb, rows):
    depth = rpb.shape[0]
    cidx = np.arange(GRID_W)
    col_start = np.clip(cidx - NA_WIN_COLS // 2, 0, GRID_W - NA_WIN_COLS)
    in_win = (cidx[None, :] >= col_start[:, None]) & (cidx[None, :] < col_start[:, None] + NA_WIN_COLS)
    dc = np.clip(cidx[None, :] - cidx[:, None], 1 - NA_WIN_COLS, NA_WIN_COLS - 1) + NA_WIN_COLS - 1
    pick = (dc.T[None] == np.arange(2 * NA_WIN_COLS - 1)[:, None, None]).astype(np.float32)
    t = jnp.einsum('ldc,ckq->ldkq', rpb.reshape(depth * N_HEADS, *rpb.shape[2:]), pick,
                   precision=lax.Precision.HIGHEST)
    t = jnp.where(in_win.T[None, None], t * LOG2E, MASKED)
    outside = jnp.full_like(t[:, 0], MASKED)
    nblk = rows // NA_Q_ROWS
    cats = []
    for blk in (0, 1, nblk - 1):
        band0 = int(np.clip(blk * NA_Q_ROWS - NA_WIN_ROWS // 2, 0, rows - NA_BAND_ROWS))
        key_rows = []
        for kj in range(NA_BAND_ROWS):
            kr = band0 + kj
            per_q = []
            for qi in range(NA_Q_ROWS):
                r = blk * NA_Q_ROWS + qi
                r0 = int(np.clip(r - NA_WIN_ROWS // 2, 0, rows - NA_WIN_ROWS))
                per_q.append(t[:, kr - r + NA_WIN_ROWS - 1] if r0 <= kr < r0 + NA_WIN_ROWS else outside)
            key_rows.append(jnp.concatenate(per_q, axis=-1))
        cats.append(jnp.concatenate(key_rows, axis=1))
    return jnp.stack(cats).reshape(3 * depth, N_HEADS, NA_BAND_ROWS * GRID_W, NA_Q_ROWS * GRID_W)


def _sgu_kernel(u_ref, v_ref, w_ref, b_ref, o_ref):
    tm = u_ref.shape[1]
    lane = lax.broadcasted_iota(jnp.int32, (SGU_CHUNK, GROUP_W), 1)
    for c in range(tm // SGU_CHUNK):
        rows = slice(c * SGU_CHUNK, (c + 1) * SGU_CHUNK)
        vchunk = v_ref[0, rows, :]
        s = b_ref[...]
        for g in range(N_HEADS):
            in_group = (lane >= g * HEAD_DIM) & (lane < (g + 1) * HEAD_DIM)
            s = jnp.where(in_group, s + _dot(w_ref[g], vchunk), s)
        o_ref[0, rows, :] = (u_ref[0, rows, :].astype(f32) * s).astype(o_ref.dtype)


def _sgu(layer, u, v, w, b_full):
    bsz, s, _ = u.shape
    tm = min(s, 512)
    row = pl.BlockSpec((1, tm, GROUP_W), lambda b, i: (b, i, 0))
    return pl.pallas_call(
        _sgu_kernel,
        grid=(bsz, s // tm),
        in_specs=[row, row,
                  pl.BlockSpec((None, N_HEADS, SGU_CHUNK, SGU_CHUNK), lambda b, i: (layer, 0, 0, 0)),
                  pl.BlockSpec((None, SGU_CHUNK, GROUP_W), lambda b, i: (layer, 0, 0))],
        out_specs=row,
        out_shape=jax.ShapeDtypeStruct((bsz, s, GROUP_W), bf16),
        name="sgu",
    )(u, v, w, b_full)


def _attn_t_kernel(*refs, layer, n_maps, post_scale, sources):
    lam_ref, q_ref = refs[:2]
    pos = 2
    band0 = pl.multiple_of(_na_band_start(pl.program_id(1)) * GRID_W, KEY_CHUNK)
    chunks = []
    for n_keys, banded, biased in sources:
        k_ref, vt_ref = refs[pos:pos + 2]
        bias_ref = refs[pos + 2] if biased else None
        pos += 3 if biased else 2
        for off in range(0, n_keys, KEY_CHUNK):
            chunks.append((k_ref, vt_ref, bias_ref, pl.multiple_of(band0 + off, KEY_CHUNK) if banded else off, off))
    g_ref, o_ref, qm_ref, sa_ref, sb_ref, ma_ref, mb_ref, acc_ref = refs[pos:]

    n_streams = N_HEADS * n_maps
    width = GROUP_W // n_streams
    tq = q_ref.shape[1]
    qt = q_ref[0].astype(f32).T.astype(bf16)
    chan = lax.broadcasted_iota(jnp.int32, qt.shape, 0)
    for j in range(n_streams):
        qm_ref[j] = jnp.where((chan >= j * width) & (chan < (j + 1) * width), qt, jnp.zeros_like(qt))

    def step(j_scores, j_values, to_first):
        s_new, m_new = (sa_ref, ma_ref) if to_first else (sb_ref, mb_ref)
        s_old, m_old = (sb_ref, mb_ref) if to_first else (sa_ref, ma_ref)
        if j_scores is not None:
            mx = jnp.full((8, tq), MASKED, f32)
        if j_values is not None:
            m = m_old[...]
            acc = jnp.zeros((2 * HEAD_DIM, tq), f32)
        for c, (k_ref, vt_ref, bias_ref, first, bias_first) in enumerate(chunks):
            rows = slice(c * KEY_CHUNK, (c + 1) * KEY_CHUNK)
            if j_scores is not None:
                s = _dot(k_ref[0, pl.ds(first, KEY_CHUNK), :], qm_ref[j_scores])
                if bias_ref is not None:
                    s = s + bias_ref[j_scores // n_maps, bias_first:bias_first + KEY_CHUNK, :]
                s_new[rows, :] = s
                mx = jnp.maximum(mx, jnp.max(s.reshape(KEY_CHUNK // 8, 8, tq), axis=0))
            if j_values is not None:
                p = jnp.exp2(s_old[rows, :] - m).astype(bf16)
                acc = acc + _dot(vt_ref[0, j_values // n_maps, :, pl.ds(first, KEY_CHUNK)], p)
        if j_scores is not None:
            m_new[...] = jnp.max(mx, axis=0, keepdims=True)
        if j_values is not None:
            acc_ref[j_values] = acc

    step(0, None, True)

    def stream_pair(i, carry):
        step(2 * i + 1, 2 * i, False)
        step(2 * i + 2, 2 * i + 1, True)
        return carry

    lax.fori_loop(0, n_streams // 2 - 1, stream_pair, 0)
    step(n_streams - 1, n_streams - 2, False)
    step(None, n_streams - 1, True)

    outs = []
    for h in range(N_HEADS):
        if n_maps == 1:
            a = acc_ref[h]
            o = a[:HEAD_DIM] / a[HEAD_DIM:HEAD_DIM + 1]
        else:
            a1 = acc_ref[2 * h]
            a2 = acc_ref[2 * h + 1]
            o = a1[:HEAD_DIM] / a1[HEAD_DIM:HEAD_DIM + 1] - lam_ref[layer] * (a2[:HEAD_DIM] / a2[HEAD_DIM:HEAD_DIM + 1])
            r = lax.rsqrt(jnp.mean(o * o, axis=0, keepdims=True) + EPS)
            o = (o * r) * g_ref[...] * post_scale
        outs.append(o)
    o_ref[0] = jnp.concatenate(outs, axis=0).T.astype(o_ref.dtype)


def _attn_t(name, layer, q, srcs, lam, g_rows, n_maps, post_scale):
    bsz, s, _ = q.shape
    tq = min(s, 512)
    nblk = s // tq
    n_streams = N_HEADS * n_maps
    n_total = sum(src[3] for src in srcs)
    in_specs = [
        pl.BlockSpec(memory_space=pltpu.SMEM),
        pl.BlockSpec((1, tq, GROUP_W), lambda b, i: (b, i, 0)),
    ]
    args = [lam, q]
    for k, vt, bias, n_used, banded in srcs:
        in_specs += [
            pl.BlockSpec((1, k.shape[1], GROUP_W), lambda b, i: (b, 0, 0)),
            pl.BlockSpec((1, N_HEADS, 2 * HEAD_DIM, k.shape[1]), lambda b, i: (b, 0, 0, 0)),
        ]
        args += [k, vt]
        if bias is not None:
            n_layers = bias.shape[0] // 3
            in_specs.append(pl.BlockSpec(
                (None, N_HEADS, n_used, tq),
                lambda b, i: (jnp.where(i == 0, 0, jnp.where(i == nblk - 1, 2, 1)) * n_layers + layer, 0, 0, 0)))
            args.append(bias)
    in_specs.append(pl.BlockSpec((None, HEAD_DIM, tq), lambda b, i: (layer, 0, 0)))
    args.append(g_rows)
    return pl.pallas_call(
        partial(_attn_t_kernel, layer=layer, n_maps=n_maps, post_scale=post_scale,
                sources=tuple((src[3], src[4], src[2] is not None) for src in srcs)),
        grid=(bsz, nblk),
        in_specs=in_specs,
        out_specs=pl.BlockSpec((1, tq, GROUP_W), lambda b, i: (b, i, 0)),
        out_shape=jax.ShapeDtypeStruct((bsz, s, GROUP_W), bf16),
        scratch_shapes=[
            pltpu.VMEM((n_streams, GROUP_W, tq), bf16),
            pltpu.VMEM((n_total, tq), f32), pltpu.VMEM((n_total, tq), f32),
            pltpu.VMEM((1, tq), f32), pltpu.VMEM((1, tq), f32),
            pltpu.VMEM((n_streams, 2 * HEAD_DIM, tq), f32),
        ],
        compiler_params=pltpu.CompilerParams(vmem_limit_bytes=V7X_VMEM_LIMIT),
        name=name,
    )(*args)


def _post_kernel(x_ref, ya_ref, yb_ref, yc_ref, yd_ref, g1_ref, sh_ref, sc_ref, g2_ref, ng_ref, wo_ref, w1_ref,
                 w2_ref, fg_ref, o_ref, *, final):
    y = jnp.concatenate([ya_ref[0], yb_ref[0], yc_ref[0], yd_ref[0]], axis=-1)
    x = x_ref[0] + g1_ref[0] * _dot(y, wo_ref[...])
    r = lax.rsqrt(jnp.mean(x * x, axis=-1, keepdims=True) + EPS)
    h = (x * r) * ng_ref[...]
    hb = (h * (1.0 + sc_ref[0]) + sh_ref[0]).astype(bf16)
    acc = jnp.zeros(x.shape, f32)
    step = D_MODEL
    for j in range(D_FF // step):
        t = jnp.maximum(_dot(hb, w1_ref[:, j * step:(j + 1) * step]), 0.0)
        acc = acc + _dot((t * t).astype(bf16), w2_ref[j * step:(j + 1) * step, :])
    x = x + g2_ref[0] * acc
    if final:
        r = lax.rsqrt(jnp.mean(x * x, axis=-1, keepdims=True) + EPS)
        x = (x * r) * fg_ref[...]
    o_ref[0] = x


def _post(x, ys, mods, layer, ctx, ng, wo, w1, w2, fg, final):
    bsz, s, _ = x.shape
    tm = min(s, 512)
    row = lambda width: pl.BlockSpec((1, tm, width), lambda b, i: (b, i, 0))
    of_layer = lambda shape: pl.BlockSpec((None,) + shape, lambda b, i: (layer, 0, 0), pipeline_mode=pl.Buffered(1))
    return pl.pallas_call(
        partial(_post_kernel, final=final),
        grid=(bsz, s // tm),
        in_specs=[
            row(D_MODEL), row(GROUP_W), row(GROUP_W), row(GROUP_W), row(GROUP_W),
            _mod_spec(layer, 2, ctx), _mod_spec(layer, 3, ctx), _mod_spec(layer, 4, ctx), _mod_spec(layer, 5, ctx),
            of_layer((1, D_MODEL)), of_layer((D_MODEL, D_MODEL)), of_layer((D_MODEL, D_FF)),
            of_layer((D_FF, D_MODEL)),
            pl.BlockSpec((1, D_MODEL), lambda b, i: (0, 0)),
        ],
        out_specs=row(D_MODEL),
        out_shape=jax.ShapeDtypeStruct((bsz, s, D_MODEL), f32),
        compiler_params=pltpu.CompilerParams(vmem_limit_bytes=V7X_VMEM_LIMIT),
        name="post_ctx" if ctx else "post",
    )(x, *ys, mods, mods, mods, mods, ng, wo, w1, w2, fg)


def _channel_dft(n_pos):
    c = np.arange(HEAD_DIM)
    ang = 2.0 * np.pi * ((c[:, None] * c[None, :]) % HEAD_DIM) / HEAD_DIM
    scale = 1.0 / math.sqrt(n_pos * HEAD_DIM)
    bd = np.zeros((GROUP_W, 2 * GROUP_W), np.float32)
    for g in range(N_HEADS):
        sl = slice(g * HEAD_DIM, (g + 1) * HEAD_DIM)
        bd[sl, sl] = np.cos(ang) * scale
        bd[sl, GROUP_W + g * HEAD_DIM:GROUP_W + (g + 1) * HEAD_DIM] = -np.sin(ang) * scale
    return jnp.asarray(bd, bf16)


def _position_dft(n_pos):
    hi = max(n_pos // GRID_W, 1)
    lo = n_pos // hi
    col = jnp.arange(n_pos, dtype=jnp.int32)[None, :]
    a = 2.0 * np.pi * ((col * jnp.arange(hi, dtype=jnp.int32)[:, None] * lo) % n_pos).astype(f32) / n_pos
    b = 2.0 * np.pi * ((col * jnp.arange(lo, dtype=jnp.int32)[:, None]) % n_pos).astype(f32) / n_pos
    ca, sa, cb, sb = jnp.cos(a)[:, None, :], jnp.sin(a)[:, None, :], jnp.cos(b)[None, :, :], jnp.sin(b)[None, :, :]
    cn = (ca * cb - sa * sb).reshape(n_pos, n_pos)
    sn = (sa * cb + ca * sb).reshape(n_pos, n_pos)
    return cn.astype(bf16), sn.astype(bf16)


def _rope_tables(n):
    t = jnp.arange(n)
    rows = (t // GRID_W).astype(f32)
    cols = (t % GRID_W).astype(f32)
    n_freq = DIFF_QK_DIM // 4
    inv = ROPE_BASE ** (-jnp.arange(n_freq, dtype=f32) / n_freq)
    ang = jnp.concatenate([rows[:, None] * inv, cols[:, None] * inv], axis=-1)
    lane = np.arange(GROUP_W) % DIFF_QK_DIM
    src = (lane // 16) * n_freq + lane % n_freq
    sign = np.where((lane % 16) < n_freq, -1.0, 1.0).astype(np.float32)
    return jnp.cos(ang)[:, src], jnp.sin(ang)[:, src] * sign


def kernel(x, c, ctx, c_ctx, ada_w, ada_b, norm1_g, norm2_g, w_in, w_out, na_rpb, sgu_ln_g, sgu_ln_b, sgu_w, sgu_b,
           diff_lq1, diff_lk1, diff_lq2, diff_lk2, diff_subln_g, w_ff1, w_ff2, final_g):
    bsz, n, _ = x.shape
    n_ctx = ctx.shape[1]
    depth = w_in.shape[0]
    rows = n // GRID_W
    assert bsz <= COND_ROWS // 2

    cond = jnp.zeros((COND_ROWS, D_MODEL), f32).at[:bsz].set(c).at[COND_ROWS // 2].set(c_ctx)
    mods = _ada_mods(cond, ada_w, ada_b).reshape(depth * COND_ROWS * 6, 1, D_MODEL)

    cos_l, sin_l = _rope_tables(n)
    rope_lat = (cos_l * DIFF_Q_SCALE, sin_l * DIFF_Q_SCALE, cos_l, sin_l)
    ones = jnp.ones((n_ctx, GROUP_W), f32)
    rope_ctx = (ones * DIFF_Q_SCALE, ones * 0.0, ones, ones * 0.0)
    bd_lat, bd_ctx = _channel_dft(n), _channel_dft(n_ctx)
    dft_lat, dft_ctx = _position_dft(n), _position_dft(n_ctx)

    w_in_b, w_out_b = w_in.astype(bf16), w_out.astype(bf16)
    w1_b, w2_b, sgu_w_b = w_ff1.astype(bf16), w_ff2.astype(bf16), sgu_w.astype(bf16)
    fg = final_g.reshape(1, D_MODEL)
    g1 = norm1_g.reshape(depth, 1, D_MODEL)
    g2 = norm2_g.reshape(depth, 1, D_MODEL)
    lng = sgu_ln_g.reshape(depth, 1, GROUP_W)
    lnb = sgu_ln_b.reshape(depth, 1, GROUP_W)
    sgu_bias = jnp.repeat(sgu_b.transpose(0, 2, 1), HEAD_DIM, axis=2)
    subln = jnp.broadcast_to(diff_subln_g[:, :, None], (depth, HEAD_DIM, min(n, 512)))
    lam_init = [0.8 - 0.6 * math.exp(-0.3 * l) for l in range(depth)]
    lam = (jnp.exp(jnp.sum(diff_lq1.astype(f32) * diff_lk1.astype(f32), axis=-1))
           - jnp.exp(jnp.sum(diff_lq2.astype(f32) * diff_lk2.astype(f32), axis=-1)) + jnp.asarray(lam_init, f32))
    na_bias = _na_bias_t(na_rpb, rows)
    band = NA_BAND_ROWS * GRID_W
    cx = ctx

    for l in range(depth):
        last = l == depth - 1
        post_scale = 1.0 - lam_init[l]

        p_lat = _proj_in(x, mods, l, False, g1, w_in_b, bd_lat, rope_lat, lng, lnb)
        p_ctx = _proj_in(cx, mods, l, True, g1, w_in_b, bd_ctx, rope_ctx, lng, lnb)
        acs, naq, nak, nav, sgu_u, sgu_v, dfq, dfk, dfv = p_lat
        c_acs, c_naq, c_nak, c_nav, c_sgu_u, c_sgu_v, c_dfq, c_dfk, c_dfv = p_ctx

        y_a = _fourier(*dft_lat, acs)
        ctx_na = (c_nak, c_nav, None, n_ctx, False)
        ctx_df = (c_dfk, c_dfv, None, n_ctx, False)
        y_b = _attn_t("na", l, naq, [(nak, nav, na_bias, band, True), ctx_na], lam, subln, 1, 1.0)
        y_c = _sgu(l, sgu_u, sgu_v, sgu_w_b, sgu_bias)
        y_d = _attn_t("diff", l, dfq, [(dfk, dfv, None, n, False), ctx_df], lam, subln, 2, post_scale)

        if not last:
            yc_a = _fourier(*dft_ctx, c_acs)
            yc_b = _attn_t("dense_ctx", l, c_naq, [ctx_na], lam, subln, 1, 1.0)
            yc_c = _sgu(l, c_sgu_u, c_sgu_v, sgu_w_b, sgu_bias)
            yc_d = _attn_t("diff_ctx", l, c_dfq, [ctx_df], lam, subln, 2, post_scale)
            cx = _post(cx, (yc_a, yc_b, yc_c, yc_d), mods, l, True, g2, w_out_b, w1_b, w2_b, fg, False)

        x = _post(x, (y_a, y_b, y_c, y_d), mods, l, False, g2, w_out_b, w1_b, w2_b, fg, last)
    return x
```

```python
import math
from functools import partial

import numpy as np
import jax
import jax.numpy as jnp
from jax import lax
from jax.experimental import pallas as pl
from jax.experimental.pallas import tpu as pltpu

D_MODEL = 1024
GRID_W = 64
N_HEADS = 4
GROUP_W = 256
HEAD_DIM = 64
NA_WIN_ROWS = 8
NA_WIN_COLS = 16
SGU_CHUNK = 128
DIFF_QK_DIM = 32
ROPE_BASE = 10000.0
D_FF = 4 * D_MODEL
D_IN = 9 * GROUP_W
EPS = 1e-6

LOG2E = 1.4426950408889634
NA_Q_SCALE = HEAD_DIM ** -0.5 * LOG2E
DIFF_Q_SCALE = DIFF_QK_DIM ** -0.5 * LOG2E
MASKED = -1e30
COND_ROWS = 16
LANES = 128
KEY_CHUNK = 256
NA_Q_ROWS = 8
NA_BAND_ROWS = 16
PROJ_ROWS = 1024
PROJ_SUB_ROWS = 256
V7X_VMEM_LIMIT = 56 * 1024 * 1024

f32 = jnp.float32
bf16 = jnp.bfloat16


def _dot(a, b):
    return jnp.dot(a, b, preferred_element_type=f32)


def _ada_kernel(cond_ref, w_ref, b_ref, o_ref):
    c = cond_ref[...]
    s = (c * jax.nn.sigmoid(c)).astype(bf16)
    o_ref[0] = _dot(s, w_ref[0].astype(bf16)) + b_ref[0]


def _ada_mods(cond, ada_w, ada_b):
    depth = ada_w.shape[0]
    tn = 1536
    return pl.pallas_call(
        _ada_kernel,
        grid=(depth, 6 * D_MODEL // tn),
        in_specs=[
            pl.BlockSpec((COND_ROWS, D_MODEL), lambda l, j: (0, 0)),
            pl.BlockSpec((1, D_MODEL, tn), lambda l, j: (l, 0, j)),
            pl.BlockSpec((1, 1, tn), lambda l, j: (l, 0, j)),
        ],
        out_specs=pl.BlockSpec((1, COND_ROWS, tn), lambda l, j: (l, 0, j)),
        out_shape=jax.ShapeDtypeStruct((depth, COND_ROWS, 6 * D_MODEL), f32),
        compiler_params=pltpu.CompilerParams(vmem_limit_bytes=V7X_VMEM_LIMIT),
        name="ada",
    )(cond, ada_w, ada_b.reshape(depth, 1, 6 * D_MODEL))


def _mod_spec(layer, which, ctx):
    if ctx:
        return pl.BlockSpec((1, 1, D_MODEL), lambda b, i: ((layer * COND_ROWS + COND_ROWS // 2) * 6 + which, 0, 0))
    return pl.BlockSpec((1, 1, D_MODEL), lambda b, i: ((layer * COND_ROWS + b) * 6 + which, 0, 0))


def _rope_partner(x):
    lane = lax.broadcasted_iota(jnp.int32, x.shape, 1)
    first_half = (lane % 16) < 8
    n = x.shape[1]
    return jnp.where(first_half, pltpu.roll(x, n - 8, 1), pltpu.roll(x, 8, 1))


def _values_t(v):
    vt = v.T.astype(bf16).reshape(N_HEADS, HEAD_DIM, v.shape[0])
    return jnp.concatenate([vt, jnp.ones_like(vt)], axis=1)


def _proj_in_kernel(x_ref, sh_ref, sc_ref, g_ref, w_ref, bd_ref, cq_ref, sq_ref, ck_ref, sk_ref, lng_ref, lnb_ref,
                    acs_ref, naq_ref, nak_ref, nav_ref, sgu_ref, sgv_ref, dfq_ref, dfk_ref, dfv_ref):
    gw = GROUP_W
    gain = g_ref[...] * (1.0 + sc_ref[0])
    sub = min(x_ref.shape[1], PROJ_SUB_ROWS)
    for r0 in range(0, x_ref.shape[1], sub):
        rows = slice(r0, r0 + sub)
        x = x_ref[0, rows, :]
        r = lax.rsqrt(jnp.mean(x * x, axis=-1, keepdims=True) + EPS)
        hb = ((x * r) * gain + sh_ref[0]).astype(bf16)

        def mm(col, width=gw):
            return _dot(hb, w_ref[:, col:col + width])

        acs_ref[0, rows, :] = _dot(mm(0).astype(bf16), bd_ref[...]).astype(bf16)
        q = mm(6 * gw)
        dfq_ref[0, rows, :] = (q * cq_ref[rows, :] + _rope_partner(q) * sq_ref[rows, :]).astype(bf16)
        k = mm(7 * gw)
        dfk_ref[0, rows, :] = (k * ck_ref[rows, :] + _rope_partner(k) * sk_ref[rows, :]).astype(bf16)
        dfv_ref[0, :, :, rows] = _values_t(mm(8 * gw))
        uv = jax.nn.gelu(mm(4 * gw, 2 * gw), approximate=True)
        sgu_ref[0, rows, :] = uv[:, :gw].astype(bf16)
        v = uv[:, gw:]
        mu = jnp.mean(v, axis=-1, keepdims=True)
        vc = v - mu
        var = jnp.mean(vc * vc, axis=-1, keepdims=True)
        sgv_ref[0, rows, :] = (vc * lax.rsqrt(var + EPS) * lng_ref[...] + lnb_ref[...]).astype(bf16)
        nav_ref[0, :, :, rows] = _values_t(mm(3 * gw))
        naq_ref[0, rows, :] = (mm(gw) * NA_Q_SCALE).astype(bf16)
        nak_ref[0, rows, :] = mm(2 * gw).astype(bf16)


def _proj_in(x, mods, layer, ctx, g, w, bd, rope, lng, lnb):
    bsz, s, _ = x.shape
    tm = min(s, PROJ_ROWS)
    row = lambda b, i: (b, i, 0)
    const = lambda b, i: (0, 0)
    of_layer = lambda b, i: (layer, 0, 0)
    pos = lambda b, i: (i, 0)
    out = lambda width: pl.BlockSpec((1, tm, width), row)
    shape = lambda width: jax.ShapeDtypeStruct((bsz, s, width), bf16)
    out_vt = pl.BlockSpec((1, N_HEADS, 2 * HEAD_DIM, tm), lambda b, i: (b, 0, 0, i))
    shape_vt = jax.ShapeDtypeStruct((bsz, N_HEADS, 2 * HEAD_DIM, s), bf16)
    widths = (2 * GROUP_W, GROUP_W, GROUP_W, None, GROUP_W, GROUP_W, GROUP_W, GROUP_W, None)
    return pl.pallas_call(
        _proj_in_kernel,
        grid=(bsz, s // tm),
        in_specs=[
            pl.BlockSpec((1, tm, D_MODEL), row),
            _mod_spec(layer, 0, ctx), _mod_spec(layer, 1, ctx),
            pl.BlockSpec((None, 1, D_MODEL), of_layer),
            pl.BlockSpec((None, D_MODEL, D_IN), of_layer),
            pl.BlockSpec((GROUP_W, 2 * GROUP_W), const),
            pl.BlockSpec((tm, GROUP_W), pos), pl.BlockSpec((tm, GROUP_W), pos),
            pl.BlockSpec((tm, GROUP_W), pos), pl.BlockSpec((tm, GROUP_W), pos),
            pl.BlockSpec((None, 1, GROUP_W), of_layer), pl.BlockSpec((None, 1, GROUP_W), of_layer),
        ],
        out_specs=[out_vt if wd is None else out(wd) for wd in widths],
        out_shape=[shape_vt if wd is None else shape(wd) for wd in widths],
        compiler_params=pltpu.CompilerParams(vmem_limit_bytes=V7X_VMEM_LIMIT),
        name="proj_in_ctx" if ctx else "proj_in",
    )(x, mods, mods, g, w, bd, *rope, lng, lnb)


def _fourier_kernel(c_ref, s_ref, a_ref, o_ref, acc_ref):
    k = pl.program_id(1)
    gw = GROUP_W

    @pl.when(k == 0)
    def _():
        acc_ref[...] = jnp.zeros_like(acc_ref)

    for b in range(a_ref.shape[0]):
        acc_ref[b] += _dot(c_ref[...], a_ref[b, :, :gw]) + _dot(s_ref[...], a_ref[b, :, gw:])

    @pl.when(k == pl.num_programs(1) - 1)
    def _():
        o_ref[...] = acc_ref[...].astype(o_ref.dtype)


def _fourier(cn, sn, acs):
    bsz, s, _ = acs.shape
    tm = min(s, 1024)
    tk = min(s, 512)
    return pl.pallas_call(
        _fourier_kernel,
        grid=(s // tm, s // tk),
        in_specs=[
            pl.BlockSpec((tm, tk), lambda i, k: (i, k)),
            pl.BlockSpec((tm, tk), lambda i, k: (i, k)),
            pl.BlockSpec((bsz, tk, 2 * GROUP_W), lambda i, k: (0, k, 0)),
        ],
        out_specs=pl.BlockSpec((bsz, tm, GROUP_W), lambda i, k: (0, i, 0)),
        out_shape=jax.ShapeDtypeStruct((bsz, s, GROUP_W), bf16),
        scratch_shapes=[pltpu.VMEM((bsz, tm, GROUP_W), f32)],
        compiler_params=pltpu.CompilerParams(vmem_limit_bytes=V7X_VMEM_LIMIT),
        name="fourier",
    )(cn, sn, acs)


def _fourier_fft_kernel(z_ref, w1_ref, tc_ref, ts_ref, w2_ref, o_ref, u_ref, y_ref):
    r = z_ref.shape[1]
    gw = GROUP_W

    def stage1(n2, carry):
        zc = z_ref[0, :, pl.ds(pl.multiple_of(n2 * 2 * gw, 2 * gw), 2 * gw)]
        g = _dot(w1_ref[...], zc)
        ur = g[:r, :gw] - g[r:, gw:]
        ui = g[:r, gw:] + g[r:, :gw]
        tc = tc_ref[:, pl.ds(pl.multiple_of(n2 * gw, gw), gw)]
        ts = ts_ref[:, pl.ds(pl.multiple_of(n2 * gw, gw), gw)]
        rows = pl.ds(pl.multiple_of(n2 * r, r), r)
        for j, part in enumerate((ur * tc + ui * ts, ui * tc - ur * ts)):
            u_ref[2 * j, rows, :] = part[:, :LANES]
            u_ref[2 * j + 1, rows, :] = part[:, LANES:]
        return carry

    lax.fori_loop(0, r, stage1, 0, unroll=8)

    for n1p in range(r):
        rows = pl.ds(n1p, r, stride=r)
        rhs = jnp.concatenate(
            [jnp.concatenate([u_ref[0, rows, :], u_ref[1, rows, :]], axis=1),
             jnp.concatenate([u_ref[2, rows, :], u_ref[3, rows, :]], axis=1)], axis=0).astype(bf16)
        y = _dot(w2_ref[...], rhs)
        y_ref[0, rows, :] = y[:, :LANES]
        y_ref[1, rows, :] = y[:, LANES:]
    o_ref[0] = jnp.concatenate([y_ref[0], y_ref[1]], axis=1).astype(o_ref.dtype)


def _fourier_fft(acs):
    bsz, s, _ = acs.shape
    r = GRID_W
    assert s == r * r
    k = np.arange(r)
    ang = 2.0 * np.pi * ((k[:, None] * k[None, :]) % r) / r
    w1 = jnp.asarray(np.concatenate([np.cos(ang), -np.sin(ang)], axis=0), bf16)
    w2 = jnp.asarray(np.concatenate([np.cos(ang), np.sin(ang)], axis=1), bf16)
    tw = 2.0 * np.pi * ((k[:, None] * k[None, :]) % s) / s
    tc = jnp.repeat(jnp.asarray(np.cos(tw), f32), GROUP_W, axis=1)
    ts = jnp.repeat(jnp.asarray(np.sin(tw), f32), GROUP_W, axis=1)
    const = lambda shape: pl.BlockSpec(shape, lambda b: (0, 0), pipeline_mode=pl.Buffered(1))
    return pl.pallas_call(
        _fourier_fft_kernel,
        grid=(bsz,),
        in_specs=[
            pl.BlockSpec((1, r, r * 2 * GROUP_W), lambda b: (b, 0, 0)),
            const((2 * r, r)), const((r, r * GROUP_W)), const((r, r * GROUP_W)), const((r, 2 * r)),
        ],
        out_specs=pl.BlockSpec((1, s, GROUP_W), lambda b: (b, 0, 0)),
        out_shape=jax.ShapeDtypeStruct((bsz, s, GROUP_W), bf16),
        scratch_shapes=[pltpu.VMEM((2 * GROUP_W // LANES, s, LANES), f32),
                        pltpu.VMEM((GROUP_W // LANES, s, LANES), f32)],
        compiler_params=pltpu.CompilerParams(vmem_limit_bytes=V7X_VMEM_LIMIT),
        name="fourier_fft",
    )(acs.reshape(bsz, r, r * 2 * GROUP_W), w1, tc, ts, w2)


def _na_band_start(i):
    rows = GRID_W
    return jnp.clip(i * NA_Q_ROWS - NA_WIN_ROWS // 2, 0, rows - NA_BAND_ROWS)


def _na_bias_t(rpb, rows):
    depth = rpb.shape[0]
    cidx = np.arange(GRID_W)
    col_start = np.clip(cidx - NA_WIN_COLS // 2, 0, GRID_W - NA_WIN_COLS)
    in_win = (cidx[None, :] >= col_start[:, None]) & (cidx[None, :] < col_start[:, None] + NA_WIN_COLS)
    dc = np.clip(cidx[None, :] - cidx[:, None], 1 - NA_WIN_COLS, NA_WIN_COLS - 1) + NA_WIN_COLS - 1
    pick = (dc.T[None] == np.arange(2 * NA_WIN_COLS - 1)[:, None, None]).astype(np.float32)
    t = jnp.einsum('ldc,ckq->ldkq', rpb.reshape(depth * N_HEADS, *rpb.shape[2:]), pick,
                   precision=lax.Precision.HIGHEST)
    t = jnp.where(in_win.T[None, None], t * LOG2E, MASKED)
    outside = jnp.full_like(t[:, 0], MASKED)
    nblk = rows // NA_Q_ROWS
    cats = []
    for blk in (0, 1, nblk - 1):
        band0 = int(np.clip(blk * NA_Q_ROWS - NA_WIN_ROWS // 2, 0, rows - NA_BAND_ROWS))
        key_rows = []
        for kj in range(NA_BAND_ROWS):
            kr = band0 + kj
            per_q = []
            for qi in range(NA_Q_ROWS):
                r = blk * NA_Q_ROWS + qi
                r0 = int(np.clip(r - NA_WIN_ROWS // 2, 0, rows - NA_WIN_ROWS))
                per_q.append(t[:, kr - r + NA_WIN_ROWS - 1] if r0 <= kr < r0 + NA_WIN_ROWS else outside)
            key_rows.append(jnp.concatenate(per_q, axis=-1))
        cats.append(jnp.concatenate(key_rows, axis=1))
    return jnp.stack(cats).reshape(3 * depth, N_HEADS, NA_BAND_ROWS * GRID_W, NA_Q_ROWS * GRID_W)


def _sgu_kernel(u_ref, v_ref, w_ref, b_ref, o_ref):
    tm = u_ref.shape[1]
    lane = lax.broadcasted_iota(jnp.int32, (SGU_CHUNK, GROUP_W), 1)
    for c in range(tm // SGU_CHUNK):
        rows = slice(c * SGU_CHUNK, (c + 1) * SGU_CHUNK)
        vchunk = v_ref[0, rows, :]
        s = b_ref[...]
        for g in range(N_HEADS):
            in_group = (lane >= g * HEAD_DIM) & (lane < (g + 1) * HEAD_DIM)
            s = jnp.where(in_group, s + _dot(w_ref[g], vchunk), s)
        o_ref[0, rows, :] = (u_ref[0, rows, :].astype(f32) * s).astype(o_ref.dtype)


def _sgu(layer, u, v, w, b_full):
    bsz, s, _ = u.shape
    tm = min(s, 512)
    row = pl.BlockSpec((1, tm, GROUP_W), lambda b, i: (b, i, 0))
    return pl.pallas_call(
        _sgu_kernel,
        grid=(bsz, s // tm),
        in_specs=[row, row,
                  pl.BlockSpec((None, N_HEADS, SGU_CHUNK, SGU_CHUNK), lambda b, i: (layer, 0, 0, 0)),
                  pl.BlockSpec((None, SGU_CHUNK, GROUP_W), lambda b, i: (layer, 0, 0))],
        out_specs=row,
        out_shape=jax.ShapeDtypeStruct((bsz, s, GROUP_W), bf16),
        name="sgu",
    )(u, v, w, b_full)


def _attn_t_kernel(*refs, layer, n_maps, post_scale, sources, n_blocks):
    lam_ref, q_ref = refs[:2]
    pos = 2
    i = pl.program_id(1)
    band_new = _na_band_start(jnp.minimum(i, n_blocks - 1)) * GRID_W
    band_old = _na_band_start(jnp.maximum(i - 1, 0)) * GRID_W
    chunks = []
    for n_keys, banded, biased in sources:
        k_ref, vt_ref = refs[pos:pos + 2]
        bias_ref = refs[pos + 2] if biased else None
        pos += 3 if biased else 2
        for off in range(0, n_keys, KEY_CHUNK):
            first = (pl.multiple_of(band_new + off, KEY_CHUNK), pl.multiple_of(band_old + off, KEY_CHUNK))
            chunks.append((k_ref, vt_ref, bias_ref) + (first if banded else (off, off)) + (off,))
    g_ref, o_ref, qm_ref, sa_ref, sb_ref, ma_ref, mb_ref, acc_ref = refs[pos:]

    n_streams = N_HEADS * n_maps
    width = GROUP_W // n_streams
    tq = q_ref.shape[1]
    qt = q_ref[0].astype(f32).T.astype(bf16)
    chan = lax.broadcasted_iota(jnp.int32, qt.shape, 0)
    for j in range(n_streams):
        qm_ref[j] = jnp.where((chan >= j * width) & (chan < (j + 1) * width), qt, jnp.zeros_like(qt))

    @pl.when(i == 0)
    def _():
        sb_ref[...] = jnp.zeros_like(sb_ref)
        mb_ref[...] = jnp.zeros_like(mb_ref)
        acc_ref[...] = jnp.ones_like(acc_ref)

    def step(j_scores, j_values, to_first, values_of_old_block=False):
        s_new, m_new = (sa_ref, ma_ref) if to_first else (sb_ref, mb_ref)
        s_old, m_old = (sb_ref, mb_ref) if to_first else (sa_ref, ma_ref)
        mx = jnp.full((8, tq), MASKED, f32)
        m = m_old[...]
        acc = jnp.zeros((2 * HEAD_DIM, tq), f32)
        for c, (k_ref, vt_ref, bias_ref, first_new, first_old, bias_first) in enumerate(chunks):
            rows = slice(c * KEY_CHUNK, (c + 1) * KEY_CHUNK)
            s = _dot(k_ref[0, pl.ds(first_new, KEY_CHUNK), :], qm_ref[j_scores])
            if bias_ref is not None:
                s = s + bias_ref[j_scores // n_maps, bias_first:bias_first + KEY_CHUNK, :]
            s_new[rows, :] = s
            mx = jnp.maximum(mx, jnp.max(s.reshape(KEY_CHUNK // 8, 8, tq), axis=0))
            p = jnp.exp2(s_old[rows, :] - m).astype(bf16)
            first = first_old if values_of_old_block else first_new
            acc = acc + _dot(vt_ref[0, j_values // n_maps, :, pl.ds(first, KEY_CHUNK)], p)
        m_new[...] = jnp.max(mx, axis=0, keepdims=True)
        acc_ref[j_values] = acc

    step(0, n_streams - 1, True, values_of_old_block=True)
    _attn_finish(lam_ref, g_ref, acc_ref, o_ref, layer, n_maps, post_scale)

    @pl.when(i < n_blocks)
    def _():
        step(1, 0, False)

        def stream_pair(t, carry):
            step(2 * t + 2, 2 * t + 1, True)
            step(2 * t + 3, 2 * t + 2, False)
            return carry

        lax.fori_loop(0, n_streams // 2 - 1, stream_pair, 0)


def _attn_finish(lam_ref, g_ref, acc_ref, o_ref, layer, n_maps, post_scale):
    outs = []
    for h in range(N_HEADS):
        if n_maps == 1:
            a = acc_ref[h]
            o = a[:HEAD_DIM] / a[HEAD_DIM:HEAD_DIM + 1]
        else:
            a1 = acc_ref[2 * h]
            a2 = acc_ref[2 * h + 1]
            o = a1[:HEAD_DIM] / a1[HEAD_DIM:HEAD_DIM + 1] - lam_ref[layer] * (a2[:HEAD_DIM] / a2[HEAD_DIM:HEAD_DIM + 1])
            r = lax.rsqrt(jnp.mean(o * o, axis=0, keepdims=True) + EPS)
            o = (o * r) * g_ref[...] * post_scale
        outs.append(o)
    o_ref[0] = jnp.concatenate(outs, axis=0).T.astype(o_ref.dtype)


def _attn_t(name, layer, q, srcs, lam, g_rows, n_maps, post_scale):
    bsz, s, _ = q.shape
    tq = min(s, 512)
    nblk = s // tq
    n_streams = N_HEADS * n_maps
    n_total = sum(src[3] for src in srcs)
    in_specs = [
        pl.BlockSpec(memory_space=pltpu.SMEM),
        pl.BlockSpec((1, tq, GROUP_W), lambda b, i: (b, jnp.minimum(i, nblk - 1), 0)),
    ]
    args = [lam, q]
    for k, vt, bias, n_used, banded in srcs:
        in_specs += [
            pl.BlockSpec((1, k.shape[1], GROUP_W), lambda b, i: (b, 0, 0)),
            pl.BlockSpec((1, N_HEADS, 2 * HEAD_DIM, k.shape[1]), lambda b, i: (b, 0, 0, 0)),
        ]
        args += [k, vt]
        if bias is not None:
            n_layers = bias.shape[0] // 3
            in_specs.append(pl.BlockSpec(
                (None, N_HEADS, n_used, tq),
                lambda b, i: (jnp.where(i == 0, 0, jnp.where(i >= nblk - 1, 2, 1)) * n_layers + layer, 0, 0, 0)))
            args.append(bias)
    in_specs.append(pl.BlockSpec((None, HEAD_DIM, tq), lambda b, i: (layer, 0, 0)))
    args.append(g_rows)
    return pl.pallas_call(
        partial(_attn_t_kernel, layer=layer, n_maps=n_maps, post_scale=post_scale,
                sources=tuple((src[3], src[4], src[2] is not None) for src in srcs), n_blocks=nblk),
        grid=(bsz, nblk + 1),
        in_specs=in_specs,
        out_specs=pl.BlockSpec((1, tq, GROUP_W), lambda b, i: (b, jnp.maximum(i - 1, 0), 0)),
        out_shape=jax.ShapeDtypeStruct((bsz, s, GROUP_W), bf16),
        scratch_shapes=[
            pltpu.VMEM((n_streams, GROUP_W, tq), bf16),
            pltpu.VMEM((n_total, tq), f32), pltpu.VMEM((n_total, tq), f32),
            pltpu.VMEM((1, tq), f32), pltpu.VMEM((1, tq), f32),
            pltpu.VMEM((n_streams, 2 * HEAD_DIM, tq), f32),
        ],
        compiler_params=pltpu.CompilerParams(vmem_limit_bytes=V7X_VMEM_LIMIT),
        name=name,
    )(*args)


def _post_kernel(x_ref, ya_ref, yb_ref, yc_ref, yd_ref, g1_ref, sh_ref, sc_ref, g2_ref, ng_ref, wo_ref, w1_ref,
                 w2_ref, fg_ref, o_ref, *, final):
    y = jnp.concatenate([ya_ref[0], yb_ref[0], yc_ref[0], yd_ref[0]], axis=-1)
    x = x_ref[0] + g1_ref[0] * _dot(y, wo_ref[...])
    r = lax.rsqrt(jnp.mean(x * x, axis=-1, keepdims=True) + EPS)
    h = (x * r) * ng_ref[...]
    hb = (h * (1.0 + sc_ref[0]) + sh_ref[0]).astype(bf16)
    acc = jnp.zeros(x.shape, f32)
    step = D_MODEL
    for j in range(D_FF // step):
        t = jnp.maximum(_dot(hb, w1_ref[:, j * step:(j + 1) * step]), 0.0)
        acc = acc + _dot((t * t).astype(bf16), w2_ref[j * step:(j + 1) * step, :])
    x = x + g2_ref[0] * acc
    if final:
        r = lax.rsqrt(jnp.mean(x * x, axis=-1, keepdims=True) + EPS)
        x = (x * r) * fg_ref[...]
    o_ref[0] = x


def _post(x, ys, mods, layer, ctx, ng, wo, w1, w2, fg, final):
    bsz, s, _ = x.shape
    tm = min(s, 512)
    row = lambda width: pl.BlockSpec((1, tm, width), lambda b, i: (b, i, 0))
    of_layer = lambda shape: pl.BlockSpec((None,) + shape, lambda b, i: (layer, 0, 0), pipeline_mode=pl.Buffered(1))
    return pl.pallas_call(
        partial(_post_kernel, final=final),
        grid=(bsz, s // tm),
        in_specs=[
            row(D_MODEL), row(GROUP_W), row(GROUP_W), row(GROUP_W), row(GROUP_W),
            _mod_spec(layer, 2, ctx), _mod_spec(layer, 3, ctx), _mod_spec(layer, 4, ctx), _mod_spec(layer, 5, ctx),
            of_layer((1, D_MODEL)), of_layer((D_MODEL, D_MODEL)), of_layer((D_MODEL, D_FF)),
            of_layer((D_FF, D_MODEL)),
            pl.BlockSpec((1, D_MODEL), lambda b, i: (0, 0)),
        ],
        out_specs=row(D_MODEL),
        out_shape=jax.ShapeDtypeStruct((bsz, s, D_MODEL), f32),
        compiler_params=pltpu.CompilerParams(vmem_limit_bytes=V7X_VMEM_LIMIT),
        name="post_ctx" if ctx else "post",
    )(x, *ys, mods, mods, mods, mods, ng, wo, w1, w2, fg)


def _channel_dft(n_pos):
    c = np.arange(HEAD_DIM)
    ang = 2.0 * np.pi * ((c[:, None] * c[None, :]) % HEAD_DIM) / HEAD_DIM
    scale = 1.0 / math.sqrt(n_pos * HEAD_DIM)
    bd = np.zeros((GROUP_W, 2 * GROUP_W), np.float32)
    for g in range(N_HEADS):
        sl = slice(g * HEAD_DIM, (g + 1) * HEAD_DIM)
        bd[sl, sl] = np.cos(ang) * scale
        bd[sl, GROUP_W + g * HEAD_DIM:GROUP_W + (g + 1) * HEAD_DIM] = -np.sin(ang) * scale
    return jnp.asarray(bd, bf16)


def _position_dft(n_pos):
    hi = max(n_pos // GRID_W, 1)
    lo = n_pos // hi
    col = jnp.arange(n_pos, dtype=jnp.int32)[None, :]
    a = 2.0 * np.pi * ((col * jnp.arange(hi, dtype=jnp.int32)[:, None] * lo) % n_pos).astype(f32) / n_pos
    b = 2.0 * np.pi * ((col * jnp.arange(lo, dtype=jnp.int32)[:, None]) % n_pos).astype(f32) / n_pos
    ca, sa, cb, sb = jnp.cos(a)[:, None, :], jnp.sin(a)[:, None, :], jnp.cos(b)[None, :, :], jnp.sin(b)[None, :, :]
    cn = (ca * cb - sa * sb).reshape(n_pos, n_pos)
    sn = (sa * cb + ca * sb).reshape(n_pos, n_pos)
    return cn.astype(bf16), sn.astype(bf16)


def _rope_tables(n):
    t = jnp.arange(n)
    rows = (t // GRID_W).astype(f32)
    cols = (t % GRID_W).astype(f32)
    n_freq = DIFF_QK_DIM // 4
    inv = ROPE_BASE ** (-jnp.arange(n_freq, dtype=f32) / n_freq)
    ang = jnp.concatenate([rows[:, None] * inv, cols[:, None] * inv], axis=-1)
    lane = np.arange(GROUP_W) % DIFF_QK_DIM
    src = (lane // 16) * n_freq + lane % n_freq
    sign = np.where((lane % 16) < n_freq, -1.0, 1.0).astype(np.float32)
    return jnp.cos(ang)[:, src], jnp.sin(ang)[:, src] * sign


def kernel(x, c, ctx, c_ctx, ada_w, ada_b, norm1_g, norm2_g, w_in, w_out, na_rpb, sgu_ln_g, sgu_ln_b, sgu_w, sgu_b,
           diff_lq1, diff_lk1, diff_lq2, diff_lk2, diff_subln_g, w_ff1, w_ff2, final_g):
    bsz, n, _ = x.shape
    n_ctx = ctx.shape[1]
    depth = w_in.shape[0]
    rows = n // GRID_W
    assert bsz <= COND_ROWS // 2

    cond = jnp.zeros((COND_ROWS, D_MODEL), f32).at[:bsz].set(c).at[COND_ROWS // 2].set(c_ctx)
    mods = _ada_mods(cond, ada_w, ada_b).reshape(depth * COND_ROWS * 6, 1, D_MODEL)

    cos_l, sin_l = _rope_tables(n)
    rope_lat = (cos_l * DIFF_Q_SCALE, sin_l * DIFF_Q_SCALE, cos_l, sin_l)
    ones = jnp.ones((n_ctx, GROUP_W), f32)
    rope_ctx = (ones * DIFF_Q_SCALE, ones * 0.0, ones, ones * 0.0)
    bd_lat, bd_ctx = _channel_dft(n), _channel_dft(n_ctx)
    dft_ctx = _position_dft(n_ctx)

    w_in_b, w_out_b = w_in.astype(bf16), w_out.astype(bf16)
    w1_b, w2_b, sgu_w_b = w_ff1.astype(bf16), w_ff2.astype(bf16), sgu_w.astype(bf16)
    fg = final_g.reshape(1, D_MODEL)
    g1 = norm1_g.reshape(depth, 1, D_MODEL)
    g2 = norm2_g.reshape(depth, 1, D_MODEL)
    lng = sgu_ln_g.reshape(depth, 1, GROUP_W)
    lnb = sgu_ln_b.reshape(depth, 1, GROUP_W)
    sgu_bias = jnp.repeat(sgu_b.transpose(0, 2, 1), HEAD_DIM, axis=2)
    subln = jnp.broadcast_to(diff_subln_g[:, :, None], (depth, HEAD_DIM, min(n, 512)))
    lam_init = [0.8 - 0.6 * math.exp(-0.3 * l) for l in range(depth)]
    lam = (jnp.exp(jnp.sum(diff_lq1.astype(f32) * diff_lk1.astype(f32), axis=-1))
           - jnp.exp(jnp.sum(diff_lq2.astype(f32) * diff_lk2.astype(f32), axis=-1)) + jnp.asarray(lam_init, f32))
    na_bias = _na_bias_t(na_rpb, rows)
    band = NA_BAND_ROWS * GRID_W
    cx = ctx

    for l in range(depth):
        last = l == depth - 1
        post_scale = 1.0 - lam_init[l]

        p_lat = _proj_in(x, mods, l, False, g1, w_in_b, bd_lat, rope_lat, lng, lnb)
        p_ctx = _proj_in(cx, mods, l, True, g1, w_in_b, bd_ctx, rope_ctx, lng, lnb)
        acs, naq, nak, nav, sgu_u, sgu_v, dfq, dfk, dfv = p_lat
        c_acs, c_naq, c_nak, c_nav, c_sgu_u, c_sgu_v, c_dfq, c_dfk, c_dfv = p_ctx

        y_a = _fourier_fft(acs)
        ctx_na = (c_nak, c_nav, None, n_ctx, False)
        ctx_df = (c_dfk, c_dfv, None, n_ctx, False)
        y_b = _attn_t("na", l, naq, [(nak, nav, na_bias, band, True), ctx_na], lam, subln, 1, 1.0)
        y_c = _sgu(l, sgu_u, sgu_v, sgu_w_b, sgu_bias)
        y_d = _attn_t("diff", l, dfq, [(dfk, dfv, None, n, False), ctx_df], lam, subln, 2, post_scale)

        if not last:
            yc_a = _fourier(*dft_ctx, c_acs)
            yc_b = _attn_t("dense_ctx", l, c_naq, [ctx_na], lam, subln, 1, 1.0)
            yc_c = _sgu(l, c_sgu_u, c_sgu_v, sgu_w_b, sgu_bias)
            yc_d = _attn_t("diff_ctx", l, c_dfq, [ctx_df], lam, subln, 2, post_scale)
            cx = _post(cx, (yc_a, yc_b, yc_c, yc_d), mods, l, True, g2, w_out_b, w1_b, w2_b, fg, False)

        x = _post(x, (y_a, y_b, y_c, y_d), mods, l, False, g2, w_out_b, w1_b, w2_b, fg, last)
    return x
```

```python
import math
from functools import partial

import numpy as np
import jax
import jax.numpy as jnp
from jax import lax
from jax.experimental import pallas as pl
from jax.experimental.pallas import tpu as pltpu

D_MODEL = 1024
GRID_W = 64
N_HEADS = 4
GROUP_W = 256
HEAD_DIM = 64
NA_WIN_ROWS = 8
NA_WIN_COLS = 16
SGU_CHUNK = 128
DIFF_QK_DIM = 32
ROPE_BASE = 10000.0
D_FF = 4 * D_MODEL
D_IN = 9 * GROUP_W
EPS = 1e-6

LOG2E = 1.4426950408889634
NA_Q_SCALE = HEAD_DIM ** -0.5 * LOG2E
DIFF_Q_SCALE = DIFF_QK_DIM ** -0.5 * LOG2E
MASKED = -1e30
COND_ROWS = 16
KEY_CHUNK = 256
NA_Q_ROWS = 8
NA_BAND_ROWS = 16
PROJ_ROWS = 1024
PROJ_SUB_ROWS = 256
V7X_VMEM_LIMIT = 56 * 1024 * 1024

FP8_STREAM_LANES = 4 * DIFF_QK_DIM

f32 = jnp.float32
bf16 = jnp.bfloat16
fp8 = jnp.float8_e4m3fn


def _dot(a, b):
    return jnp.dot(a, b, preferred_element_type=f32)


def _ada_kernel(cond_ref, w_ref, b_ref, o_ref):
    c = cond_ref[...]
    s = (c * jax.nn.sigmoid(c)).astype(bf16)
    o_ref[0] = _dot(s, w_ref[0].astype(bf16)) + b_ref[0]


def _ada_mods(cond, ada_w, ada_b):
    depth = ada_w.shape[0]
    tn = 1536
    return pl.pallas_call(
        _ada_kernel,
        grid=(depth, 6 * D_MODEL // tn),
        in_specs=[
            pl.BlockSpec((COND_ROWS, D_MODEL), lambda l, j: (0, 0)),
            pl.BlockSpec((1, D_MODEL, tn), lambda l, j: (l, 0, j)),
            pl.BlockSpec((1, 1, tn), lambda l, j: (l, 0, j)),
        ],
        out_specs=pl.BlockSpec((1, COND_ROWS, tn), lambda l, j: (l, 0, j)),
        out_shape=jax.ShapeDtypeStruct((depth, COND_ROWS, 6 * D_MODEL), f32),
        compiler_params=pltpu.CompilerParams(vmem_limit_bytes=V7X_VMEM_LIMIT),
        name="ada",
    )(cond, ada_w, ada_b.reshape(depth, 1, 6 * D_MODEL))


def _mod_spec(layer, which, ctx):
    if ctx:
        return pl.BlockSpec((1, 1, D_MODEL), lambda b, i: ((layer * COND_ROWS + COND_ROWS // 2) * 6 + which, 0, 0))
    return pl.BlockSpec((1, 1, D_MODEL), lambda b, i: ((layer * COND_ROWS + b) * 6 + which, 0, 0))


def _rope_partner(x):
    lane = lax.broadcasted_iota(jnp.int32, x.shape, 1)
    first_half = (lane % 16) < 8
    n = x.shape[1]
    return jnp.where(first_half, pltpu.roll(x, n - 8, 1), pltpu.roll(x, 8, 1))


def _split_fp8(x):
    hi = x.astype(fp8)
    return hi, (x - hi.astype(f32)).astype(fp8)


def _keys_fp8(k):
    rows = k.shape[0]
    hi, lo = (pltpu.bitcast(part, jnp.uint32) for part in _split_fp8(k))
    slot = lax.broadcasted_iota(jnp.int32, (rows // 4, FP8_STREAM_LANES), 1) // DIFF_QK_DIM
    blocks = []
    for j in range(GROUP_W // DIFF_QK_DIM):
        half = slice(j // 4 * FP8_STREAM_LANES, (j // 4 + 1) * FP8_STREAM_LANES)
        at = j % 4

        def moved(src, to):
            shift = (to - at) * DIFF_QK_DIM % FP8_STREAM_LANES
            return src[:, half] if shift == 0 else pltpu.roll(src[:, half], shift, 1)

        block = jnp.where(slot == 0, moved(hi, 0),
                          jnp.where(slot == 1, moved(hi, 1), jnp.where(slot == 2, moved(lo, 2), moved(lo, 3))))
        blocks.append(pltpu.bitcast(block, fp8))
    return jnp.concatenate(blocks, axis=1)


def _queries_fp8_t(q):
    hi, lo = _split_fp8(q.T)
    pieces = []
    for j in range(GROUP_W // DIFF_QK_DIM):
        own = slice(j * DIFF_QK_DIM, (j + 1) * DIFF_QK_DIM)
        pieces += [hi[own], lo[own], hi[own], lo[own]]
    return jnp.concatenate(pieces, axis=0)


def _values_t(v):
    vt = v.T.astype(bf16).reshape(N_HEADS, HEAD_DIM, v.shape[0])
    return jnp.concatenate([vt, jnp.ones_like(vt)], axis=1)


def _proj_in_kernel(x_ref, sh_ref, sc_ref, g_ref, w_ref, bd_ref, cq_ref, sq_ref, ck_ref, sk_ref, lng_ref, lnb_ref,
                    acs_ref, naq_ref, nak_ref, nav_ref, sgu_ref, sgv_ref, dfq_ref, dfk_ref, dfv_ref):
    gw = GROUP_W
    gain = g_ref[...] * (1.0 + sc_ref[0])
    sub = min(x_ref.shape[1], PROJ_SUB_ROWS)
    for r0 in range(0, x_ref.shape[1], sub):
        rows = slice(r0, r0 + sub)
        x = x_ref[0, rows, :]
        r = lax.rsqrt(jnp.mean(x * x, axis=-1, keepdims=True) + EPS)
        hb = ((x * r) * gain + sh_ref[0]).astype(bf16)

        def mm(col, width=gw):
            return _dot(hb, w_ref[:, col:col + width])

        acs_ref[0, rows, :] = _dot(mm(0).astype(bf16), bd_ref[...]).astype(bf16)
        q = mm(6 * gw)
        dfq_ref[0, :, rows] = _queries_fp8_t(q * cq_ref[rows, :] + _rope_partner(q) * sq_ref[rows, :])
        k = mm(7 * gw)
        dfk_ref[0, rows, :] = _keys_fp8(k * ck_ref[rows, :] + _rope_partner(k) * sk_ref[rows, :])
        dfv_ref[0, :, :, rows] = _values_t(mm(8 * gw))
        uv = jax.nn.gelu(mm(4 * gw, 2 * gw), approximate=True)
        sgu_ref[0, rows, :] = uv[:, :gw].astype(bf16)
        v = uv[:, gw:]
        mu = jnp.mean(v, axis=-1, keepdims=True)
        vc = v - mu
        var = jnp.mean(vc * vc, axis=-1, keepdims=True)
        sgv_ref[0, rows, :] = (vc * lax.rsqrt(var + EPS) * lng_ref[...] + lnb_ref[...]).astype(bf16)
        nav_ref[0, :, :, rows] = _values_t(mm(3 * gw))
        naq_ref[0, :, rows] = (mm(gw) * NA_Q_SCALE).T.astype(bf16)
        nak_ref[0, rows, :] = mm(2 * gw).astype(bf16)


def _proj_in(x, mods, layer, ctx, g, w, bd, rope, lng, lnb):
    bsz, s, _ = x.shape
    tm = min(s, PROJ_ROWS)
    row = lambda b, i: (b, i, 0)
    const = lambda b, i: (0, 0)
    of_layer = lambda b, i: (layer, 0, 0)
    pos = lambda b, i: (i, 0)
    out = lambda width: pl.BlockSpec((1, tm, width), row)
    shape = lambda width: jax.ShapeDtypeStruct((bsz, s, width), bf16)
    vt = (pl.BlockSpec((1, N_HEADS, 2 * HEAD_DIM, tm), lambda b, i: (b, 0, 0, i)),
          jax.ShapeDtypeStruct((bsz, N_HEADS, 2 * HEAD_DIM, s), bf16))
    qt = (pl.BlockSpec((1, GROUP_W, tm), lambda b, i: (b, 0, i)), jax.ShapeDtypeStruct((bsz, GROUP_W, s), bf16))
    by_rows = lambda width: (out(width), shape(width))
    n_fp8 = GROUP_W // DIFF_QK_DIM * FP8_STREAM_LANES
    q8t = (pl.BlockSpec((1, n_fp8, tm), lambda b, i: (b, 0, i)), jax.ShapeDtypeStruct((bsz, n_fp8, s), fp8))
    k8 = (pl.BlockSpec((1, tm, n_fp8), row), jax.ShapeDtypeStruct((bsz, s, n_fp8), fp8))
    outs = (by_rows(2 * GROUP_W), qt, by_rows(GROUP_W), vt, by_rows(GROUP_W), by_rows(GROUP_W), q8t, k8, vt)
    return pl.pallas_call(
        _proj_in_kernel,
        grid=(bsz, s // tm),
        in_specs=[
            pl.BlockSpec((1, tm, D_MODEL), row),
            _mod_spec(layer, 0, ctx), _mod_spec(layer, 1, ctx),
            pl.BlockSpec((None, 1, D_MODEL), of_layer),
            pl.BlockSpec((None, D_MODEL, D_IN), of_layer),
            pl.BlockSpec((GROUP_W, 2 * GROUP_W), const),
            pl.BlockSpec((tm, GROUP_W), pos), pl.BlockSpec((tm, GROUP_W), pos),
            pl.BlockSpec((tm, GROUP_W), pos), pl.BlockSpec((tm, GROUP_W), pos),
            pl.BlockSpec((None, 1, GROUP_W), of_layer), pl.BlockSpec((None, 1, GROUP_W), of_layer),
        ],
        out_specs=[spec for spec, _ in outs],
        out_shape=[shp for _, shp in outs],
        compiler_params=pltpu.CompilerParams(vmem_limit_bytes=V7X_VMEM_LIMIT),
        name="proj_in_ctx" if ctx else "proj_in",
    )(x, mods, mods, g, w, bd, *rope, lng, lnb)


def _fourier_kernel(c_ref, s_ref, a_ref, o_ref, acc_ref):
    k = pl.program_id(1)
    gw = GROUP_W

    @pl.when(k == 0)
    def _():
        acc_ref[...] = jnp.zeros_like(acc_ref)

    for b in range(a_ref.shape[0]):
        acc_ref[b] += _dot(c_ref[...], a_ref[b, :, :gw]) + _dot(s_ref[...], a_ref[b, :, gw:])

    @pl.when(k == pl.num_programs(1) - 1)
    def _():
        o_ref[...] = acc_ref[...].astype(o_ref.dtype)


def _fourier(cn, sn, acs):
    bsz, s, _ = acs.shape
    tm = min(s, 1024)
    tk = min(s, 512)
    return pl.pallas_call(
        _fourier_kernel,
        grid=(s // tm, s // tk),
        in_specs=[
            pl.BlockSpec((tm, tk), lambda i, k: (i, k)),
            pl.BlockSpec((tm, tk), lambda i, k: (i, k)),
            pl.BlockSpec((bsz, tk, 2 * GROUP_W), lambda i, k: (0, k, 0)),
        ],
        out_specs=pl.BlockSpec((bsz, tm, GROUP_W), lambda i, k: (0, i, 0)),
        out_shape=jax.ShapeDtypeStruct((bsz, s, GROUP_W), bf16),
        scratch_shapes=[pltpu.VMEM((bsz, tm, GROUP_W), f32)],
        compiler_params=pltpu.CompilerParams(vmem_limit_bytes=V7X_VMEM_LIMIT),
        name="fourier",
    )(cn, sn, acs)


def _na_band_start(i):
    rows = GRID_W
    return jnp.clip(i * NA_Q_ROWS - NA_WIN_ROWS // 2, 0, rows - NA_BAND_ROWS)


def _na_bias_t(rpb, rows):
    depth = rpb.shape[0]
    cidx = np.arange(GRID_W)
    col_start = np.clip(cidx - NA_WIN_COLS // 2, 0, GRID_W - NA_WIN_COLS)
    in_win = (cidx[None, :] >= col_start[:, None]) & (cidx[None, :] < col_start[:, None] + NA_WIN_COLS)
    dc = np.clip(cidx[None, :] - cidx[:, None], 1 - NA_WIN_COLS, NA_WIN_COLS - 1) + NA_WIN_COLS - 1
    pick = (dc.T[None] == np.arange(2 * NA_WIN_COLS - 1)[:, None, None]).astype(np.float32)
    t = jnp.einsum('ldc,ckq->ldkq', rpb.reshape(depth * N_HEADS, *rpb.shape[2:]), pick,
                   precision=lax.Precision.HIGHEST)
    t = jnp.where(in_win.T[None, None], t * LOG2E, MASKED)
    outside = jnp.full_like(t[:, 0], MASKED)
    nblk = rows // NA_Q_ROWS
    cats = []
    for blk in (0, 1, nblk - 1):
        band0 = int(np.clip(blk * NA_Q_ROWS - NA_WIN_ROWS // 2, 0, rows - NA_BAND_ROWS))
        key_rows = []
        for kj in range(NA_BAND_ROWS):
            kr = band0 + kj
            per_q = []
            for qi in range(NA_Q_ROWS):
                r = blk * NA_Q_ROWS + qi
                r0 = int(np.clip(r - NA_WIN_ROWS // 2, 0, rows - NA_WIN_ROWS))
                per_q.append(t[:, kr - r + NA_WIN_ROWS - 1] if r0 <= kr < r0 + NA_WIN_ROWS else outside)
            key_rows.append(jnp.concatenate(per_q, axis=-1))
        cats.append(jnp.concatenate(key_rows, axis=1))
    return jnp.stack(cats).reshape(3 * depth, N_HEADS, NA_BAND_ROWS * GRID_W, NA_Q_ROWS * GRID_W)


def _sgu_kernel(u_ref, v_ref, w_ref, b_ref, o_ref):
    tm = u_ref.shape[1]
    lane = lax.broadcasted_iota(jnp.int32, (SGU_CHUNK, GROUP_W), 1)
    for c in range(tm // SGU_CHUNK):
        rows = slice(c * SGU_CHUNK, (c + 1) * SGU_CHUNK)
        vchunk = v_ref[0, rows, :]
        s = b_ref[...]
        for g in range(N_HEADS):
            in_group = (lane >= g * HEAD_DIM) & (lane < (g + 1) * HEAD_DIM)
            s = jnp.where(in_group, s + _dot(w_ref[g], vchunk), s)
        o_ref[0, rows, :] = (u_ref[0, rows, :].astype(f32) * s).astype(o_ref.dtype)


def _sgu(layer, u, v, w, b_full):
    bsz, s, _ = u.shape
    tm = min(s, 512)
    row = pl.BlockSpec((1, tm, GROUP_W), lambda b, i: (b, i, 0))
    return pl.pallas_call(
        _sgu_kernel,
        grid=(bsz, s // tm),
        in_specs=[row, row,
                  pl.BlockSpec((None, N_HEADS, SGU_CHUNK, SGU_CHUNK), lambda b, i: (layer, 0, 0, 0)),
                  pl.BlockSpec((None, SGU_CHUNK, GROUP_W), lambda b, i: (layer, 0, 0))],
        out_specs=row,
        out_shape=jax.ShapeDtypeStruct((bsz, s, GROUP_W), bf16),
        name="sgu",
    )(u, v, w, b_full)


def _attn_t_kernel(*refs, layer, n_maps, post_scale, sources, n_blocks):
    lam_ref, q_ref = refs[:2]
    pos = 2
    i = pl.program_id(1)
    band_new = _na_band_start(jnp.minimum(i, n_blocks - 1)) * GRID_W
    band_old = _na_band_start(jnp.maximum(i - 1, 0)) * GRID_W
    chunks = []
    for n_keys, banded, biased in sources:
        k_ref, vt_ref = refs[pos:pos + 2]
        bias_ref = refs[pos + 2] if biased else None
        pos += 3 if biased else 2
        for off in range(0, n_keys, KEY_CHUNK):
            first = (pl.multiple_of(band_new + off, KEY_CHUNK), pl.multiple_of(band_old + off, KEY_CHUNK))
            chunks.append((k_ref, vt_ref, bias_ref) + (first if banded else (off, off)) + (off,))
    g_ref, o_ref, qm_ref, sa_ref, sb_ref, ma_ref, mb_ref, acc_ref = refs[pos:]

    n_streams = N_HEADS * n_maps
    width = GROUP_W // n_streams
    tq = q_ref.shape[2]
    split = q_ref.dtype == fp8
    if not split:
        qt = q_ref[0]
        chan = lax.broadcasted_iota(jnp.int32, qt.shape, 0)
        for j in range(n_streams):
            qm_ref[j] = jnp.where((chan >= j * width) & (chan < (j + 1) * width), qt, jnp.zeros_like(qt))

    def score_operands(k_ref, first, j):
        if split:
            start = j * FP8_STREAM_LANES
            own = pl.ds(start if isinstance(j, int) else pl.multiple_of(start, FP8_STREAM_LANES), FP8_STREAM_LANES)
            return k_ref[0, pl.ds(first, KEY_CHUNK), own], q_ref[0, own, :]
        return k_ref[0, pl.ds(first, KEY_CHUNK), :], qm_ref[j]

    @pl.when(i == 0)
    def _():
        sb_ref[...] = jnp.zeros_like(sb_ref)
        mb_ref[...] = jnp.zeros_like(mb_ref)
        acc_ref[...] = jnp.ones_like(acc_ref)

    def step(j_scores, j_values, to_first, values_of_old_block=False):
        s_new, m_new = (sa_ref, ma_ref) if to_first else (sb_ref, mb_ref)
        s_old, m_old = (sb_ref, mb_ref) if to_first else (sa_ref, ma_ref)
        mx = jnp.full((8, tq), MASKED, f32)
        m = m_old[...]
        acc = jnp.zeros((2 * HEAD_DIM, tq), f32)
        for c, (k_ref, vt_ref, bias_ref, first_new, first_old, bias_first) in enumerate(chunks):
            rows = slice(c * KEY_CHUNK, (c + 1) * KEY_CHUNK)
            s = _dot(*score_operands(k_ref, first_new, j_scores))
            if bias_ref is not None:
                s = s + bias_ref[j_scores // n_maps, bias_first:bias_first + KEY_CHUNK, :]
            s_new[rows, :] = s
            mx = jnp.maximum(mx, jnp.max(s.reshape(KEY_CHUNK // 8, 8, tq), axis=0))
            p = jnp.exp2(s_old[rows, :] - m).astype(bf16)
            first = first_old if values_of_old_block else first_new
            acc = acc + _dot(vt_ref[0, j_values // n_maps, :, pl.ds(first, KEY_CHUNK)], p)
        m_new[...] = jnp.max(mx, axis=0, keepdims=True)
        acc_ref[j_values] = acc

    step(0, n_streams - 1, True, values_of_old_block=True)
    _attn_finish(lam_ref, g_ref, acc_ref, o_ref, layer, n_maps, post_scale)

    @pl.when(i < n_blocks)
    def _():
        step(1, 0, False)

        def stream_pair(t, carry):
            step(2 * t + 2, 2 * t + 1, True)
            step(2 * t + 3, 2 * t + 2, False)
            return carry

        lax.fori_loop(0, n_streams // 2 - 1, stream_pair, 0)


def _attn_finish(lam_ref, g_ref, acc_ref, o_ref, layer, n_maps, post_scale):
    outs = []
    for h in range(N_HEADS):
        if n_maps == 1:
            a = acc_ref[h]
            o = a[:HEAD_DIM] / a[HEAD_DIM:HEAD_DIM + 1]
        else:
            a1 = acc_ref[2 * h]
            a2 = acc_ref[2 * h + 1]
            o = a1[:HEAD_DIM] / a1[HEAD_DIM:HEAD_DIM + 1] - lam_ref[layer] * (a2[:HEAD_DIM] / a2[HEAD_DIM:HEAD_DIM + 1])
            r = lax.rsqrt(jnp.mean(o * o, axis=0, keepdims=True) + EPS)
            o = (o * r) * g_ref[...] * post_scale
        outs.append(o)
    o_ref[0] = jnp.concatenate(outs, axis=0).T.astype(o_ref.dtype)


def _attn_t(name, layer, q, srcs, lam, g_rows, n_maps, post_scale):
    bsz, q_rows, s = q.shape
    tq = min(s, 512)
    nblk = s // tq
    n_streams = N_HEADS * n_maps
    n_total = sum(src[3] for src in srcs)
    in_specs = [
        pl.BlockSpec(memory_space=pltpu.SMEM),
        pl.BlockSpec((1, q_rows, tq), lambda b, i: (b, 0, jnp.minimum(i, nblk - 1))),
    ]
    args = [lam, q]
    for k, vt, bias, n_used, banded in srcs:
        in_specs += [
            pl.BlockSpec((1,) + k.shape[1:], lambda b, i: (b, 0, 0)),
            pl.BlockSpec((1, N_HEADS, 2 * HEAD_DIM, k.shape[1]), lambda b, i: (b, 0, 0, 0)),
        ]
        args += [k, vt]
        if bias is not None:
            n_layers = bias.shape[0] // 3
            in_specs.append(pl.BlockSpec(
                (None, N_HEADS, n_used, tq),
                lambda b, i: (jnp.where(i == 0, 0, jnp.where(i >= nblk - 1, 2, 1)) * n_layers + layer, 0, 0, 0)))
            args.append(bias)
    in_specs.append(pl.BlockSpec((None, HEAD_DIM, tq), lambda b, i: (layer, 0, 0)))
    args.append(g_rows)
    return pl.pallas_call(
        partial(_attn_t_kernel, layer=layer, n_maps=n_maps, post_scale=post_scale,
                sources=tuple((src[3], src[4], src[2] is not None) for src in srcs), n_blocks=nblk),
        grid=(bsz, nblk + 1),
        in_specs=in_specs,
        out_specs=pl.BlockSpec((1, tq, GROUP_W), lambda b, i: (b, jnp.maximum(i - 1, 0), 0)),
        out_shape=jax.ShapeDtypeStruct((bsz, s, GROUP_W), bf16),
        scratch_shapes=[
            pltpu.VMEM((n_streams, GROUP_W, tq) if q.dtype == bf16 else (1, 16, 128), bf16),
            pltpu.VMEM((n_total, tq), f32), pltpu.VMEM((n_total, tq), f32),
            pltpu.VMEM((1, tq), f32), pltpu.VMEM((1, tq), f32),
            pltpu.VMEM((n_streams, 2 * HEAD_DIM, tq), f32),
        ],
        compiler_params=pltpu.CompilerParams(vmem_limit_bytes=V7X_VMEM_LIMIT),
        name=name,
    )(*args)


def _post_kernel(x_ref, ya_ref, yb_ref, yc_ref, yd_ref, g1_ref, sh_ref, sc_ref, g2_ref, ng_ref, wo_ref, w1_ref,
                 w2_ref, fg_ref, o_ref, *, final):
    y = jnp.concatenate([ya_ref[0], yb_ref[0], yc_ref[0], yd_ref[0]], axis=-1)
    x = x_ref[0] + g1_ref[0] * _dot(y, wo_ref[...])
    r = lax.rsqrt(jnp.mean(x * x, axis=-1, keepdims=True) + EPS)
    h = (x * r) * ng_ref[...]
    hb = (h * (1.0 + sc_ref[0]) + sh_ref[0]).astype(bf16)
    acc = jnp.zeros(x.shape, f32)
    step = D_MODEL
    for j in range(D_FF // step):
        t = jnp.maximum(_dot(hb, w1_ref[:, j * step:(j + 1) * step]), 0.0)
        acc = acc + _dot((t * t).astype(bf16), w2_ref[j * step:(j + 1) * step, :])
    x = x + g2_ref[0] * acc
    if final:
        r = lax.rsqrt(jnp.mean(x * x, axis=-1, keepdims=True) + EPS)
        x = (x * r) * fg_ref[...]
    o_ref[0] = x


def _post(x, ys, mods, layer, ctx, ng, wo, w1, w2, fg, final):
    bsz, s, _ = x.shape
    tm = min(s, 512)
    row = lambda width: pl.BlockSpec((1, tm, width), lambda b, i: (b, i, 0))
    of_layer = lambda shape: pl.BlockSpec((None,) + shape, lambda b, i: (layer, 0, 0), pipeline_mode=pl.Buffered(1))
    return pl.pallas_call(
        partial(_post_kernel, final=final),
        grid=(bsz, s // tm),
        in_specs=[
            row(D_MODEL), row(GROUP_W), row(GROUP_W), row(GROUP_W), row(GROUP_W),
            _mod_spec(layer, 2, ctx), _mod_spec(layer, 3, ctx), _mod_spec(layer, 4, ctx), _mod_spec(layer, 5, ctx),
            of_layer((1, D_MODEL)), of_layer((D_MODEL, D_MODEL)), of_layer((D_MODEL, D_FF)),
            of_layer((D_FF, D_MODEL)),
            pl.BlockSpec((1, D_MODEL), lambda b, i: (0, 0)),
        ],
        out_specs=row(D_MODEL),
        out_shape=jax.ShapeDtypeStruct((bsz, s, D_MODEL), f32),
        compiler_params=pltpu.CompilerParams(vmem_limit_bytes=V7X_VMEM_LIMIT),
        name="post_ctx" if ctx else "post",
    )(x, *ys, mods, mods, mods, mods, ng, wo, w1, w2, fg)


def _channel_dft(n_pos):
    c = np.arange(HEAD_DIM)
    ang = 2.0 * np.pi * ((c[:, None] * c[None, :]) % HEAD_DIM) / HEAD_DIM
    scale = 1.0 / math.sqrt(n_pos * HEAD_DIM)
    bd = np.zeros((GROUP_W, 2 * GROUP_W), np.float32)
    for g in range(N_HEADS):
        sl = slice(g * HEAD_DIM, (g + 1) * HEAD_DIM)
        bd[sl, sl] = np.cos(ang) * scale
        bd[sl, GROUP_W + g * HEAD_DIM:GROUP_W + (g + 1) * HEAD_DIM] = -np.sin(ang) * scale
    return jnp.asarray(bd, bf16)


def _position_dft(n_pos):
    hi = max(n_pos // GRID_W, 1)
    lo = n_pos // hi
    col = jnp.arange(n_pos, dtype=jnp.int32)[None, :]
    a = 2.0 * np.pi * ((col * jnp.arange(hi, dtype=jnp.int32)[:, None] * lo) % n_pos).astype(f32) / n_pos
    b = 2.0 * np.pi * ((col * jnp.arange(lo, dtype=jnp.int32)[:, None]) % n_pos).astype(f32) / n_pos
    ca, sa, cb, sb = jnp.cos(a)[:, None, :], jnp.sin(a)[:, None, :], jnp.cos(b)[None, :, :], jnp.sin(b)[None, :, :]
    cn = (ca * cb - sa * sb).reshape(n_pos, n_pos)
    sn = (sa * cb + ca * sb).reshape(n_pos, n_pos)
    return cn.astype(bf16), sn.astype(bf16)


def _rope_tables(n):
    t = jnp.arange(n)
    rows = (t // GRID_W).astype(f32)
    cols = (t % GRID_W).astype(f32)
    n_freq = DIFF_QK_DIM // 4
    inv = ROPE_BASE ** (-jnp.arange(n_freq, dtype=f32) / n_freq)
    ang = jnp.concatenate([rows[:, None] * inv, cols[:, None] * inv], axis=-1)
    lane = np.arange(GROUP_W) % DIFF_QK_DIM
    src = (lane // 16) * n_freq + lane % n_freq
    sign = np.where((lane % 16) < n_freq, -1.0, 1.0).astype(np.float32)
    return jnp.cos(ang)[:, src], jnp.sin(ang)[:, src] * sign


def kernel(x, c, ctx, c_ctx, ada_w, ada_b, norm1_g, norm2_g, w_in, w_out, na_rpb, sgu_ln_g, sgu_ln_b, sgu_w, sgu_b,
           diff_lq1, diff_lk1, diff_lq2, diff_lk2, diff_subln_g, w_ff1, w_ff2, final_g):
    bsz, n, _ = x.shape
    n_ctx = ctx.shape[1]
    depth = w_in.shape[0]
    rows = n // GRID_W
    assert bsz <= COND_ROWS // 2

    cond = jnp.zeros((COND_ROWS, D_MODEL), f32).at[:bsz].set(c).at[COND_ROWS // 2].set(c_ctx)
    mods = _ada_mods(cond, ada_w, ada_b).reshape(depth * COND_ROWS * 6, 1, D_MODEL)

    cos_l, sin_l = _rope_tables(n)
    rope_lat = (cos_l * DIFF_Q_SCALE, sin_l * DIFF_Q_SCALE, cos_l, sin_l)
    ones = jnp.ones((n_ctx, GROUP_W), f32)
    rope_ctx = (ones * DIFF_Q_SCALE, ones * 0.0, ones, ones * 0.0)
    bd_lat, bd_ctx = _channel_dft(n), _channel_dft(n_ctx)
    dft_lat, dft_ctx = _position_dft(n), _position_dft(n_ctx)

    w_in_b, w_out_b = w_in.astype(bf16), w_out.astype(bf16)
    w1_b, w2_b, sgu_w_b = w_ff1.astype(bf16), w_ff2.astype(bf16), sgu_w.astype(bf16)
    fg = final_g.reshape(1, D_MODEL)
    g1 = norm1_g.reshape(depth, 1, D_MODEL)
    g2 = norm2_g.reshape(depth, 1, D_MODEL)
    lng = sgu_ln_g.reshape(depth, 1, GROUP_W)
    lnb = sgu_ln_b.reshape(depth, 1, GROUP_W)
    sgu_bias = jnp.repeat(sgu_b.transpose(0, 2, 1), HEAD_DIM, axis=2)
    subln = jnp.broadcast_to(diff_subln_g[:, :, None], (depth, HEAD_DIM, min(n, 512)))
    lam_init = [0.8 - 0.6 * math.exp(-0.3 * l) for l in range(depth)]
    lam = (jnp.exp(jnp.sum(diff_lq1.astype(f32) * diff_lk1.astype(f32), axis=-1))
           - jnp.exp(jnp.sum(diff_lq2.astype(f32) * diff_lk2.astype(f32), axis=-1)) + jnp.asarray(lam_init, f32))
    na_bias = _na_bias_t(na_rpb, rows)
    band = NA_BAND_ROWS * GRID_W
    cx = ctx

    for l in range(depth):
        last = l == depth - 1
        post_scale = 1.0 - lam_init[l]

        p_lat = _proj_in(x, mods, l, False, g1, w_in_b, bd_lat, rope_lat, lng, lnb)
        p_ctx = _proj_in(cx, mods, l, True, g1, w_in_b, bd_ctx, rope_ctx, lng, lnb)
        acs, naq, nak, nav, sgu_u, sgu_v, dfq, dfk, dfv = p_lat
        c_acs, c_naq, c_nak, c_nav, c_sgu_u, c_sgu_v, c_dfq, c_dfk, c_dfv = p_ctx

        y_a = _fourier(*dft_lat, acs)
        ctx_na = (c_nak, c_nav, None, n_ctx, False)
        ctx_df = (c_dfk, c_dfv, None, n_ctx, False)
        y_b = _attn_t("na", l, naq, [(nak, nav, na_bias, band, True), ctx_na], lam, subln, 1, 1.0)
        y_c = _sgu(l, sgu_u, sgu_v, sgu_w_b, sgu_bias)
        y_d = _attn_t("diff", l, dfq, [(dfk, dfv, None, n, False), ctx_df], lam, subln, 2, post_scale)

        if not last:
            yc_a = _fourier(*dft_ctx, c_acs)
            yc_b = _attn_t("dense_ctx", l, c_naq, [ctx_na], lam, subln, 1, 1.0)
            yc_c = _sgu(l, c_sgu_u, c_sgu_v, sgu_w_b, sgu_bias)
            yc_d = _attn_t("diff_ctx", l, c_dfq, [ctx_df], lam, subln, 2, post_scale)
            cx = _post(cx, (yc_a, yc_b, yc_c, yc_d), mods, l, True, g2, w_out_b, w1_b, w2_b, fg, False)

        x = _post(x, (y_a, y_b, y_c, y_d), mods, l, False, g2, w_out_b, w1_b, w2_b, fg, last)
    return x
```

```python
import math
from functools import partial

import numpy as np
import jax
import jax.numpy as jnp
from jax import lax
from jax.experimental import pallas as pl
from jax.experimental.pallas import tpu as pltpu

D_MODEL = 1024
GRID_W = 64
N_HEADS = 4
GROUP_W = 256
HEAD_DIM = 64
NA_WIN_ROWS = 8
NA_WIN_COLS = 16
SGU_CHUNK = 128
DIFF_QK_DIM = 32
ROPE_BASE = 10000.0
D_FF = 4 * D_MODEL
D_IN = 9 * GROUP_W
EPS = 1e-6

LOG2E = 1.4426950408889634
NA_Q_SCALE = HEAD_DIM ** -0.5 * LOG2E
DIFF_Q_SCALE = DIFF_QK_DIM ** -0.5 * LOG2E
MASKED = -1e30
COND_ROWS = 16
KEY_CHUNK = 256
NA_Q_ROWS = 8
NA_BAND_ROWS = 16
ATTN_Q_BLOCK = NA_Q_ROWS * GRID_W
PROJ_ROWS = 1024
PROJ_SUB_ROWS = 256
ROW_TILE = 512
ADA_COLS = 1536
DFT_ROWS, DFT_COLS = 1024, 512
V7X_VMEM_LIMIT = 56 * 1024 * 1024

FP8_STREAM_LANES = 4 * DIFF_QK_DIM

f32 = jnp.float32
bf16 = jnp.bfloat16
fp8 = jnp.float8_e4m3fn


def _dot(a, b):
    return jnp.dot(a, b, preferred_element_type=f32)


def _ada_kernel(cond_ref, w_ref, b_ref, o_ref):
    c = cond_ref[...]
    s = (c * jax.nn.sigmoid(c)).astype(bf16)
    o_ref[0] = _dot(s, w_ref[0].astype(bf16)) + b_ref[0]


def _ada_mods(cond, ada_w, ada_b):
    depth = ada_w.shape[0]
    tn = ADA_COLS
    return pl.pallas_call(
        _ada_kernel,
        grid=(depth, 6 * D_MODEL // tn),
        in_specs=[
            pl.BlockSpec((COND_ROWS, D_MODEL), lambda l, j: (0, 0)),
            pl.BlockSpec((1, D_MODEL, tn), lambda l, j: (l, 0, j)),
            pl.BlockSpec((1, 1, tn), lambda l, j: (l, 0, j)),
        ],
        out_specs=pl.BlockSpec((1, COND_ROWS, tn), lambda l, j: (l, 0, j)),
        out_shape=jax.ShapeDtypeStruct((depth, COND_ROWS, 6 * D_MODEL), f32),
        compiler_params=pltpu.CompilerParams(vmem_limit_bytes=V7X_VMEM_LIMIT),
        name="ada",
    )(cond, ada_w, ada_b.reshape(depth, 1, 6 * D_MODEL))


def _mod_spec(layer, which, ctx):
    if ctx:
        return pl.BlockSpec((1, 1, D_MODEL), lambda b, i: ((layer * COND_ROWS + COND_ROWS // 2) * 6 + which, 0, 0))
    return pl.BlockSpec((1, 1, D_MODEL), lambda b, i: ((layer * COND_ROWS + b) * 6 + which, 0, 0))


def _rope_partner(x):
    lane = lax.broadcasted_iota(jnp.int32, x.shape, 1)
    first_half = (lane % 16) < 8
    n = x.shape[1]
    return jnp.where(first_half, pltpu.roll(x, n - 8, 1), pltpu.roll(x, 8, 1))


def _split_fp8(x):
    hi = x.astype(fp8)
    return hi, (x - hi.astype(f32)).astype(fp8)


def _keys_fp8(k):
    rows = k.shape[0]
    hi, lo = (pltpu.bitcast(part, jnp.uint32) for part in _split_fp8(k))
    slot = lax.broadcasted_iota(jnp.int32, (rows // 4, FP8_STREAM_LANES), 1) // DIFF_QK_DIM
    blocks = []
    for j in range(GROUP_W // DIFF_QK_DIM):
        half = slice(j // 4 * FP8_STREAM_LANES, (j // 4 + 1) * FP8_STREAM_LANES)
        at = j % 4

        def moved(src, to):
            shift = (to - at) * DIFF_QK_DIM % FP8_STREAM_LANES
            return src[:, half] if shift == 0 else pltpu.roll(src[:, half], shift, 1)

        block = jnp.where(slot == 0, moved(hi, 0),
                          jnp.where(slot == 1, moved(hi, 1), jnp.where(slot == 2, moved(lo, 2), moved(lo, 3))))
        blocks.append(pltpu.bitcast(block, fp8))
    return jnp.concatenate(blocks, axis=1)


def _queries_fp8_t(q):
    hi, lo = _split_fp8(q.T)
    pieces = []
    for j in range(GROUP_W // DIFF_QK_DIM):
        own = slice(j * DIFF_QK_DIM, (j + 1) * DIFF_QK_DIM)
        pieces += [hi[own], lo[own], hi[own], lo[own]]
    return jnp.concatenate(pieces, axis=0)


def _values_t(v):
    vt = v.T.astype(bf16).reshape(N_HEADS, HEAD_DIM, v.shape[0])
    return jnp.concatenate([vt, jnp.ones_like(vt)], axis=1)


def _proj_in_kernel(x_ref, sh_ref, sc_ref, g_ref, w_ref, bd_ref, cq_ref, sq_ref, ck_ref, sk_ref, lng_ref, lnb_ref,
                    acs_ref, naq_ref, nak_ref, nav_ref, sgu_ref, sgv_ref, dfq_ref, dfk_ref, dfv_ref):
    gw = GROUP_W
    gain = g_ref[...] * (1.0 + sc_ref[0])
    sub = min(x_ref.shape[1], PROJ_SUB_ROWS)
    for r0 in range(0, x_ref.shape[1], sub):
        rows = slice(r0, r0 + sub)
        x = x_ref[0, rows, :]
        r = lax.rsqrt(jnp.mean(x * x, axis=-1, keepdims=True) + EPS)
        hb = ((x * r) * gain + sh_ref[0]).astype(bf16)

        def mm(col, width=gw):
            return _dot(hb, w_ref[:, col:col + width])

        acs_ref[0, rows, :] = _dot(mm(0).astype(bf16), bd_ref[...]).astype(bf16)
        q = mm(6 * gw)
        dfq_ref[0, :, rows] = _queries_fp8_t(q * cq_ref[rows, :] + _rope_partner(q) * sq_ref[rows, :])
        k = mm(7 * gw)
        dfk_ref[0, rows, :] = _keys_fp8(k * ck_ref[rows, :] + _rope_partner(k) * sk_ref[rows, :])
        dfv_ref[0, :, :, rows] = _values_t(mm(8 * gw))
        uv = jax.nn.gelu(mm(4 * gw, 2 * gw), approximate=True)
        sgu_ref[0, rows, :] = uv[:, :gw].astype(bf16)
        v = uv[:, gw:]
        mu = jnp.mean(v, axis=-1, keepdims=True)
        vc = v - mu
        var = jnp.mean(vc * vc, axis=-1, keepdims=True)
        sgv_ref[0, rows, :] = (vc * lax.rsqrt(var + EPS) * lng_ref[...] + lnb_ref[...]).astype(bf16)
        nav_ref[0, :, :, rows] = _values_t(mm(3 * gw))
        naq_ref[0, :, rows] = (mm(gw) * NA_Q_SCALE).T.astype(bf16)
        nak_ref[0, rows, :] = mm(2 * gw).astype(bf16)


def _proj_in(x, mods, layer, ctx, g, w, bd, rope, lng, lnb):
    bsz, s, _ = x.shape
    tm = min(s, PROJ_ROWS)
    row = lambda b, i: (b, i, 0)
    const = lambda b, i: (0, 0)
    of_layer = lambda b, i: (layer, 0, 0)
    pos = lambda b, i: (i, 0)
    out = lambda width: pl.BlockSpec((1, tm, width), row)
    shape = lambda width: jax.ShapeDtypeStruct((bsz, s, width), bf16)
    vt = (pl.BlockSpec((1, N_HEADS, 2 * HEAD_DIM, tm), lambda b, i: (b, 0, 0, i)),
          jax.ShapeDtypeStruct((bsz, N_HEADS, 2 * HEAD_DIM, s), bf16))
    qt = (pl.BlockSpec((1, GROUP_W, tm), lambda b, i: (b, 0, i)), jax.ShapeDtypeStruct((bsz, GROUP_W, s), bf16))
    by_rows = lambda width: (out(width), shape(width))
    n_fp8 = GROUP_W // DIFF_QK_DIM * FP8_STREAM_LANES
    q8t = (pl.BlockSpec((1, n_fp8, tm), lambda b, i: (b, 0, i)), jax.ShapeDtypeStruct((bsz, n_fp8, s), fp8))
    k8 = (pl.BlockSpec((1, tm, n_fp8), row), jax.ShapeDtypeStruct((bsz, s, n_fp8), fp8))
    outs = (by_rows(2 * GROUP_W), qt, by_rows(GROUP_W), vt, by_rows(GROUP_W), by_rows(GROUP_W), q8t, k8, vt)
    return pl.pallas_call(
        _proj_in_kernel,
        grid=(bsz, s // tm),
        in_specs=[
            pl.BlockSpec((1, tm, D_MODEL), row),
            _mod_spec(layer, 0, ctx), _mod_spec(layer, 1, ctx),
            pl.BlockSpec((None, 1, D_MODEL), of_layer),
            pl.BlockSpec((None, D_MODEL, D_IN), of_layer),
            pl.BlockSpec((GROUP_W, 2 * GROUP_W), const),
            pl.BlockSpec((tm, GROUP_W), pos), pl.BlockSpec((tm, GROUP_W), pos),
            pl.BlockSpec((tm, GROUP_W), pos), pl.BlockSpec((tm, GROUP_W), pos),
            pl.BlockSpec((None, 1, GROUP_W), of_layer), pl.BlockSpec((None, 1, GROUP_W), of_layer),
        ],
        out_specs=[spec for spec, _ in outs],
        out_shape=[shp for _, shp in outs],
        compiler_params=pltpu.CompilerParams(vmem_limit_bytes=V7X_VMEM_LIMIT),
        name="proj_in_ctx" if ctx else "proj_in",
    )(x, mods, mods, g, w, bd, *rope, lng, lnb)


def _fourier_kernel(c_ref, s_ref, a_ref, o_ref, acc_ref):
    k = pl.program_id(1)
    gw = GROUP_W

    @pl.when(k == 0)
    def _():
        acc_ref[...] = jnp.zeros_like(acc_ref)

    for b in range(a_ref.shape[0]):
        acc_ref[b] += _dot(c_ref[...], a_ref[b, :, :gw]) + _dot(s_ref[...], a_ref[b, :, gw:])

    @pl.when(k == pl.num_programs(1) - 1)
    def _():
        o_ref[...] = acc_ref[...].astype(o_ref.dtype)


def _fourier(cn, sn, acs):
    bsz, s, _ = acs.shape
    tm = min(s, DFT_ROWS)
    tk = min(s, DFT_COLS)
    return pl.pallas_call(
        _fourier_kernel,
        grid=(s // tm, s // tk),
        in_specs=[
            pl.BlockSpec((tm, tk), lambda i, k: (i, k)),
            pl.BlockSpec((tm, tk), lambda i, k: (i, k)),
            pl.BlockSpec((bsz, tk, 2 * GROUP_W), lambda i, k: (0, k, 0)),
        ],
        out_specs=pl.BlockSpec((bsz, tm, GROUP_W), lambda i, k: (0, i, 0)),
        out_shape=jax.ShapeDtypeStruct((bsz, s, GROUP_W), bf16),
        scratch_shapes=[pltpu.VMEM((bsz, tm, GROUP_W), f32)],
        compiler_params=pltpu.CompilerParams(vmem_limit_bytes=V7X_VMEM_LIMIT),
        name="fourier",
    )(cn, sn, acs)


def _na_band_start(i):
    rows = GRID_W
    return jnp.clip(i * NA_Q_ROWS - NA_WIN_ROWS // 2, 0, rows - NA_BAND_ROWS)


def _na_bias_t(rpb, rows):
    depth = rpb.shape[0]
    cidx = np.arange(GRID_W)
    col_start = np.clip(cidx - NA_WIN_COLS // 2, 0, GRID_W - NA_WIN_COLS)
    in_win = (cidx[None, :] >= col_start[:, None]) & (cidx[None, :] < col_start[:, None] + NA_WIN_COLS)
    dc = np.clip(cidx[None, :] - cidx[:, None], 1 - NA_WIN_COLS, NA_WIN_COLS - 1) + NA_WIN_COLS - 1
    pick = (dc.T[None] == np.arange(2 * NA_WIN_COLS - 1)[:, None, None]).astype(np.float32)
    t = jnp.einsum('ldc,ckq->ldkq', rpb.reshape(depth * N_HEADS, *rpb.shape[2:]), pick,
                   precision=lax.Precision.HIGHEST)
    t = jnp.where(in_win.T[None, None], t * LOG2E, MASKED)
    outside = jnp.full_like(t[:, 0], MASKED)
    nblk = rows // NA_Q_ROWS
    cats = []
    for blk in (0, 1, nblk - 1):
        band0 = int(np.clip(blk * NA_Q_ROWS - NA_WIN_ROWS // 2, 0, rows - NA_BAND_ROWS))
        key_rows = []
        for kj in range(NA_BAND_ROWS):
            kr = band0 + kj
            per_q = []
            for qi in range(NA_Q_ROWS):
                r = blk * NA_Q_ROWS + qi
                r0 = int(np.clip(r - NA_WIN_ROWS // 2, 0, rows - NA_WIN_ROWS))
                per_q.append(t[:, kr - r + NA_WIN_ROWS - 1] if r0 <= kr < r0 + NA_WIN_ROWS else outside)
            key_rows.append(jnp.concatenate(per_q, axis=-1))
        cats.append(jnp.concatenate(key_rows, axis=1))
    return jnp.stack(cats).reshape(3 * depth, N_HEADS, NA_BAND_ROWS * GRID_W, NA_Q_ROWS * GRID_W)


def _sgu_kernel(u_ref, v_ref, w_ref, b_ref, o_ref):
    tm = u_ref.shape[1]
    lane = lax.broadcasted_iota(jnp.int32, (SGU_CHUNK, GROUP_W), 1)
    for c in range(tm // SGU_CHUNK):
        rows = slice(c * SGU_CHUNK, (c + 1) * SGU_CHUNK)
        vchunk = v_ref[0, rows, :]
        s = b_ref[...]
        for g in range(N_HEADS):
            in_group = (lane >= g * HEAD_DIM) & (lane < (g + 1) * HEAD_DIM)
            s = jnp.where(in_group, s + _dot(w_ref[g], vchunk), s)
        o_ref[0, rows, :] = (u_ref[0, rows, :].astype(f32) * s).astype(o_ref.dtype)


def _sgu(layer, u, v, w, b_full):
    bsz, s, _ = u.shape
    tm = min(s, ROW_TILE)
    row = pl.BlockSpec((1, tm, GROUP_W), lambda b, i: (b, i, 0))
    return pl.pallas_call(
        _sgu_kernel,
        grid=(bsz, s // tm),
        in_specs=[row, row,
                  pl.BlockSpec((None, N_HEADS, SGU_CHUNK, SGU_CHUNK), lambda b, i: (layer, 0, 0, 0)),
                  pl.BlockSpec((None, SGU_CHUNK, GROUP_W), lambda b, i: (layer, 0, 0))],
        out_specs=row,
        out_shape=jax.ShapeDtypeStruct((bsz, s, GROUP_W), bf16),
        name="sgu",
    )(u, v, w, b_full)


def _attn_t_kernel(*refs, layer, n_maps, post_scale, sources, n_blocks):
    lam_ref, q_ref = refs[:2]
    pos = 2
    i = pl.program_id(1)
    band_new = _na_band_start(jnp.minimum(i, n_blocks - 1)) * GRID_W
    band_old = _na_band_start(jnp.maximum(i - 1, 0)) * GRID_W
    chunks = []
    for n_keys, banded, biased in sources:
        k_ref, vt_ref = refs[pos:pos + 2]
        bias_ref = refs[pos + 2] if biased else None
        pos += 3 if biased else 2
        for off in range(0, n_keys, KEY_CHUNK):
            first = (pl.multiple_of(band_new + off, KEY_CHUNK), pl.multiple_of(band_old + off, KEY_CHUNK))
            chunks.append((k_ref, vt_ref, bias_ref) + (first if banded else (off, off)) + (off,))
    g_ref, o_ref, qm_ref, sa_ref, sb_ref, ma_ref, mb_ref, acc_ref = refs[pos:]

    n_streams = N_HEADS * n_maps
    width = GROUP_W // n_streams
    tq = q_ref.shape[2]
    split = q_ref.dtype == fp8
    if not split:
        qt = q_ref[0]
        chan = lax.broadcasted_iota(jnp.int32, qt.shape, 0)
        for j in range(n_streams):
            qm_ref[j] = jnp.where((chan >= j * width) & (chan < (j + 1) * width), qt, jnp.zeros_like(qt))

    def score_operands(k_ref, first, j):
        if split:
            start = j * FP8_STREAM_LANES
            own = pl.ds(start if isinstance(j, int) else pl.multiple_of(start, FP8_STREAM_LANES), FP8_STREAM_LANES)
            return k_ref[0, pl.ds(first, KEY_CHUNK), own], q_ref[0, own, :]
        return k_ref[0, pl.ds(first, KEY_CHUNK), :], qm_ref[j]

    @pl.when(i == 0)
    def _():
        sb_ref[...] = jnp.zeros_like(sb_ref)
        mb_ref[...] = jnp.zeros_like(mb_ref)
        acc_ref[...] = jnp.ones_like(acc_ref)

    def step(j_scores, j_values, to_first, values_of_old_block=False):
        s_new, m_new = (sa_ref, ma_ref) if to_first else (sb_ref, mb_ref)
        s_old, m_old = (sb_ref, mb_ref) if to_first else (sa_ref, ma_ref)
        mx = jnp.full((8, tq), MASKED, f32)
        m = m_old[...]
        acc = jnp.zeros((2 * HEAD_DIM, tq), f32)
        for c, (k_ref, vt_ref, bias_ref, first_new, first_old, bias_first) in enumerate(chunks):
            rows = slice(c * KEY_CHUNK, (c + 1) * KEY_CHUNK)
            s = _dot(*score_operands(k_ref, first_new, j_scores))
            if bias_ref is not None:
                s = s + bias_ref[j_scores // n_maps, bias_first:bias_first + KEY_CHUNK, :]
            s_new[rows, :] = s
            mx = jnp.maximum(mx, jnp.max(s.reshape(KEY_CHUNK // 8, 8, tq), axis=0))
            p = jnp.exp2(s_old[rows, :] - m).astype(bf16)
            first = first_old if values_of_old_block else first_new
            acc = acc + _dot(vt_ref[0, j_values // n_maps, :, pl.ds(first, KEY_CHUNK)], p)
        m_new[...] = jnp.max(mx, axis=0, keepdims=True)
        acc_ref[j_values] = acc

    step(0, n_streams - 1, True, values_of_old_block=True)
    _attn_finish(lam_ref, g_ref, acc_ref, o_ref, layer, n_maps, post_scale)

    @pl.when(i < n_blocks)
    def _():
        step(1, 0, False)

        def stream_pair(t, carry):
            step(2 * t + 2, 2 * t + 1, True)
            step(2 * t + 3, 2 * t + 2, False)
            return carry

        lax.fori_loop(0, n_streams // 2 - 1, stream_pair, 0)


def _attn_finish(lam_ref, g_ref, acc_ref, o_ref, layer, n_maps, post_scale):
    outs = []
    for h in range(N_HEADS):
        if n_maps == 1:
            a = acc_ref[h]
            o = a[:HEAD_DIM] / a[HEAD_DIM:HEAD_DIM + 1]
        else:
            a1 = acc_ref[2 * h]
            a2 = acc_ref[2 * h + 1]
            o = a1[:HEAD_DIM] / a1[HEAD_DIM:HEAD_DIM + 1] - lam_ref[layer] * (a2[:HEAD_DIM] / a2[HEAD_DIM:HEAD_DIM + 1])
            r = lax.rsqrt(jnp.mean(o * o, axis=0, keepdims=True) + EPS)
            o = (o * r) * g_ref[...] * post_scale
        outs.append(o)
    o_ref[0] = jnp.concatenate(outs, axis=0).T.astype(o_ref.dtype)


def _attn_t(name, layer, q, srcs, lam, g_rows, n_maps, post_scale):
    bsz, q_rows, s = q.shape
    tq = min(s, ATTN_Q_BLOCK)
    nblk = s // tq
    n_streams = N_HEADS * n_maps
    n_total = sum(src[3] for src in srcs)
    in_specs = [
        pl.BlockSpec(memory_space=pltpu.SMEM),
        pl.BlockSpec((1, q_rows, tq), lambda b, i: (b, 0, jnp.minimum(i, nblk - 1))),
    ]
    args = [lam, q]
    for k, vt, bias, n_used, banded in srcs:
        in_specs += [
            pl.BlockSpec((1,) + k.shape[1:], lambda b, i: (b, 0, 0)),
            pl.BlockSpec((1, N_HEADS, 2 * HEAD_DIM, k.shape[1]), lambda b, i: (b, 0, 0, 0)),
        ]
        args += [k, vt]
        if bias is not None:
            n_layers = bias.shape[0] // 3
            in_specs.append(pl.BlockSpec(
                (None, N_HEADS, n_used, tq),
                lambda b, i: (jnp.where(i == 0, 0, jnp.where(i >= nblk - 1, 2, 1)) * n_layers + layer, 0, 0, 0)))
            args.append(bias)
    in_specs.append(pl.BlockSpec((None, HEAD_DIM, tq), lambda b, i: (layer, 0, 0)))
    args.append(g_rows)
    return pl.pallas_call(
        partial(_attn_t_kernel, layer=layer, n_maps=n_maps, post_scale=post_scale,
                sources=tuple((src[3], src[4], src[2] is not None) for src in srcs), n_blocks=nblk),
        grid=(bsz, nblk + 1),
        in_specs=in_specs,
        out_specs=pl.BlockSpec((1, tq, GROUP_W), lambda b, i: (b, jnp.maximum(i - 1, 0), 0)),
        out_shape=jax.ShapeDtypeStruct((bsz, s, GROUP_W), bf16),
        scratch_shapes=[
            pltpu.VMEM((n_streams, GROUP_W, tq) if q.dtype == bf16 else (1, 16, 128), bf16),
            pltpu.VMEM((n_total, tq), f32), pltpu.VMEM((n_total, tq), f32),
            pltpu.VMEM((1, tq), f32), pltpu.VMEM((1, tq), f32),
            pltpu.VMEM((n_streams, 2 * HEAD_DIM, tq), f32),
        ],
        compiler_params=pltpu.CompilerParams(vmem_limit_bytes=V7X_VMEM_LIMIT),
        name=name,
    )(*args)


def _post_kernel(x_ref, ya_ref, yb_ref, yc_ref, yd_ref, g1_ref, sh_ref, sc_ref, g2_ref, ng_ref, wo_ref, w1_ref,
                 w2_ref, fg_ref, o_ref, *, final):
    y = jnp.concatenate([ya_ref[0], yb_ref[0], yc_ref[0], yd_ref[0]], axis=-1)
    x = x_ref[0] + g1_ref[0] * _dot(y, wo_ref[...])
    r = lax.rsqrt(jnp.mean(x * x, axis=-1, keepdims=True) + EPS)
    h = (x * r) * ng_ref[...]
    hb = (h * (1.0 + sc_ref[0]) + sh_ref[0]).astype(bf16)
    acc = jnp.zeros(x.shape, f32)
    step = D_MODEL
    for j in range(D_FF // step):
        t = jnp.maximum(_dot(hb, w1_ref[:, j * step:(j + 1) * step]), 0.0)
        acc = acc + _dot((t * t).astype(bf16), w2_ref[j * step:(j + 1) * step, :])
    x = x + g2_ref[0] * acc
    if final:
        r = lax.rsqrt(jnp.mean(x * x, axis=-1, keepdims=True) + EPS)
        x = (x * r) * fg_ref[...]
    o_ref[0] = x


def _post(x, ys, mods, layer, ctx, ng, wo, w1, w2, fg, final):
    bsz, s, _ = x.shape
    tm = min(s, ROW_TILE)
    row = lambda width: pl.BlockSpec((1, tm, width), lambda b, i: (b, i, 0))
    of_layer = lambda shape: pl.BlockSpec((None,) + shape, lambda b, i: (layer, 0, 0), pipeline_mode=pl.Buffered(1))
    return pl.pallas_call(
        partial(_post_kernel, final=final),
        grid=(bsz, s // tm),
        in_specs=[
            row(D_MODEL), row(GROUP_W), row(GROUP_W), row(GROUP_W), row(GROUP_W),
            _mod_spec(layer, 2, ctx), _mod_spec(layer, 3, ctx), _mod_spec(layer, 4, ctx), _mod_spec(layer, 5, ctx),
            of_layer((1, D_MODEL)), of_layer((D_MODEL, D_MODEL)), of_layer((D_MODEL, D_FF)),
            of_layer((D_FF, D_MODEL)),
            pl.BlockSpec((1, D_MODEL), lambda b, i: (0, 0)),
        ],
        out_specs=row(D_MODEL),
        out_shape=jax.ShapeDtypeStruct((bsz, s, D_MODEL), f32),
        compiler_params=pltpu.CompilerParams(vmem_limit_bytes=V7X_VMEM_LIMIT),
        name="post_ctx" if ctx else "post",
    )(x, *ys, mods, mods, mods, mods, ng, wo, w1, w2, fg)


def _channel_dft(n_pos):
    c = np.arange(HEAD_DIM)
    ang = 2.0 * np.pi * ((c[:, None] * c[None, :]) % HEAD_DIM) / HEAD_DIM
    scale = 1.0 / math.sqrt(n_pos * HEAD_DIM)
    bd = np.zeros((GROUP_W, 2 * GROUP_W), np.float32)
    for g in range(N_HEADS):
        sl = slice(g * HEAD_DIM, (g + 1) * HEAD_DIM)
        bd[sl, sl] = np.cos(ang) * scale
        bd[sl, GROUP_W + g * HEAD_DIM:GROUP_W + (g + 1) * HEAD_DIM] = -np.sin(ang) * scale
    return jnp.asarray(bd, bf16)


def _position_dft(n_pos):
    hi = max(n_pos // GRID_W, 1)
    lo = n_pos // hi
    col = jnp.arange(n_pos, dtype=jnp.int32)[None, :]
    a = 2.0 * np.pi * ((col * jnp.arange(hi, dtype=jnp.int32)[:, None] * lo) % n_pos).astype(f32) / n_pos
    b = 2.0 * np.pi * ((col * jnp.arange(lo, dtype=jnp.int32)[:, None]) % n_pos).astype(f32) / n_pos
    ca, sa, cb, sb = jnp.cos(a)[:, None, :], jnp.sin(a)[:, None, :], jnp.cos(b)[None, :, :], jnp.sin(b)[None, :, :]
    cn = (ca * cb - sa * sb).reshape(n_pos, n_pos)
    sn = (sa * cb + ca * sb).reshape(n_pos, n_pos)
    return cn.astype(bf16), sn.astype(bf16)


def _rope_tables(n):
    t = jnp.arange(n)
    rows = (t // GRID_W).astype(f32)
    cols = (t % GRID_W).astype(f32)
    n_freq = DIFF_QK_DIM // 4
    inv = ROPE_BASE ** (-jnp.arange(n_freq, dtype=f32) / n_freq)
    ang = jnp.concatenate([rows[:, None] * inv, cols[:, None] * inv], axis=-1)
    lane = np.arange(GROUP_W) % DIFF_QK_DIM
    src = (lane // 16) * n_freq + lane % n_freq
    sign = np.where((lane % 16) < n_freq, -1.0, 1.0).astype(np.float32)
    return jnp.cos(ang)[:, src], jnp.sin(ang)[:, src] * sign


def kernel(x, c, ctx, c_ctx, ada_w, ada_b, norm1_g, norm2_g, w_in, w_out, na_rpb, sgu_ln_g, sgu_ln_b, sgu_w, sgu_b,
           diff_lq1, diff_lk1, diff_lq2, diff_lk2, diff_subln_g, w_ff1, w_ff2, final_g):
    bsz, n, _ = x.shape
    n_ctx = ctx.shape[1]
    depth = w_in.shape[0]
    rows = n // GRID_W
    assert bsz <= COND_ROWS // 2

    cond = jnp.zeros((COND_ROWS, D_MODEL), f32).at[:bsz].set(c).at[COND_ROWS // 2].set(c_ctx)
    mods = _ada_mods(cond, ada_w, ada_b).reshape(depth * COND_ROWS * 6, 1, D_MODEL)

    cos_l, sin_l = _rope_tables(n)
    rope_lat = (cos_l * DIFF_Q_SCALE, sin_l * DIFF_Q_SCALE, cos_l, sin_l)
    ones = jnp.ones((n_ctx, GROUP_W), f32)
    rope_ctx = (ones * DIFF_Q_SCALE, ones * 0.0, ones, ones * 0.0)
    bd_lat, bd_ctx = _channel_dft(n), _channel_dft(n_ctx)
    dft_lat, dft_ctx = _position_dft(n), _position_dft(n_ctx)

    w_in_b, w_out_b = w_in.astype(bf16), w_out.astype(bf16)
    w1_b, w2_b, sgu_w_b = w_ff1.astype(bf16), w_ff2.astype(bf16), sgu_w.astype(bf16)
    fg = final_g.reshape(1, D_MODEL)
    g1 = norm1_g.reshape(depth, 1, D_MODEL)
    g2 = norm2_g.reshape(depth, 1, D_MODEL)
    lng = sgu_ln_g.reshape(depth, 1, GROUP_W)
    lnb = sgu_ln_b.reshape(depth, 1, GROUP_W)
    sgu_bias = jnp.repeat(sgu_b.transpose(0, 2, 1), HEAD_DIM, axis=2)
    subln = jnp.broadcast_to(diff_subln_g[:, :, None], (depth, HEAD_DIM, min(n, ATTN_Q_BLOCK)))
    lam_init = [0.8 - 0.6 * math.exp(-0.3 * l) for l in range(depth)]
    lam = (jnp.exp(jnp.sum(diff_lq1.astype(f32) * diff_lk1.astype(f32), axis=-1))
           - jnp.exp(jnp.sum(diff_lq2.astype(f32) * diff_lk2.astype(f32), axis=-1)) + jnp.asarray(lam_init, f32))
    na_bias = _na_bias_t(na_rpb, rows)
    band = NA_BAND_ROWS * GRID_W
    cx = ctx

    for l in range(depth):
        last = l == depth - 1
        post_scale = 1.0 - lam_init[l]

        p_lat = _proj_in(x, mods, l, False, g1, w_in_b, bd_lat, rope_lat, lng, lnb)
        p_ctx = _proj_in(cx, mods, l, True, g1, w_in_b, bd_ctx, rope_ctx, lng, lnb)
        acs, naq, nak, nav, sgu_u, sgu_v, dfq, dfk, dfv = p_lat
        c_acs, c_naq, c_nak, c_nav, c_sgu_u, c_sgu_v, c_dfq, c_dfk, c_dfv = p_ctx

        y_a = _fourier(*dft_lat, acs)
        ctx_na = (c_nak, c_nav, None, n_ctx, False)
        ctx_df = (c_dfk, c_dfv, None, n_ctx, False)
        y_b = _attn_t("na", l, naq, [(nak, nav, na_bias, band, True), ctx_na], lam, subln, 1, 1.0)
        y_c = _sgu(l, sgu_u, sgu_v, sgu_w_b, sgu_bias)
        y_d = _attn_t("diff", l, dfq, [(dfk, dfv, None, n, False), ctx_df], lam, subln, 2, post_scale)

        if not last:
            yc_a = _fourier(*dft_ctx, c_acs)
            yc_b = _attn_t("dense_ctx", l, c_naq, [ctx_na], lam, subln, 1, 1.0)
            yc_c = _sgu(l, c_sgu_u, c_sgu_v, sgu_w_b, sgu_bias)
            yc_d = _attn_t("diff_ctx", l, c_dfq, [ctx_df], lam, subln, 2, post_scale)
            cx = _post(cx, (yc_a, yc_b, yc_c, yc_d), mods, l, True, g2, w_out_b, w1_b, w2_b, fg, False)

        x = _post(x, (y_a, y_b, y_c, y_d), mods, l, False, g2, w_out_b, w1_b, w2_b, fg, last)
    return x
```

```python
import math
from functools import partial

import numpy as np
import jax
import jax.numpy as jnp
from jax import lax
from jax.experimental import pallas as pl
from jax.experimental.pallas import tpu as pltpu

D_MODEL = 1024
GRID_W = 64
N_HEADS = 4
GROUP_W = 256
HEAD_DIM = 64
NA_WIN_ROWS = 8
NA_WIN_COLS = 16
SGU_CHUNK = 128
DIFF_QK_DIM = 32
ROPE_BASE = 10000.0
D_FF = 4 * D_MODEL
D_IN = 9 * GROUP_W
EPS = 1e-6

LOG2E = 1.4426950408889634
NA_Q_SCALE = HEAD_DIM ** -0.5 * LOG2E
DIFF_Q_SCALE = DIFF_QK_DIM ** -0.5 * LOG2E
MASKED = -1e30
COND_ROWS = 16
KEY_CHUNK = 256
NA_Q_ROWS = 8
NA_BAND_ROWS = 16
ATTN_Q_BLOCK = NA_Q_ROWS * GRID_W
PROJ_ROWS = 1024
PROJ_SUB_ROWS = 256
ROW_TILE = 512
ADA_COLS = 1536
DFT_ROWS, DFT_COLS = 1024, 512
V7X_VMEM_LIMIT = 56 * 1024 * 1024

f32 = jnp.float32
bf16 = jnp.bfloat16


def _dot(a, b):
    return jnp.dot(a, b, preferred_element_type=f32)


def _ada_kernel(cond_ref, w_ref, b_ref, o_ref):
    c = cond_ref[...]
    s = (c * jax.nn.sigmoid(c)).astype(bf16)
    o_ref[0] = _dot(s, w_ref[0].astype(bf16)) + b_ref[0]


def _ada_mods(cond, ada_w, ada_b):
    depth = ada_w.shape[0]
    tn = ADA_COLS
    return pl.pallas_call(
        _ada_kernel,
        grid=(depth, 6 * D_MODEL // tn),
        in_specs=[
            pl.BlockSpec((COND_ROWS, D_MODEL), lambda l, j: (0, 0)),
            pl.BlockSpec((1, D_MODEL, tn), lambda l, j: (l, 0, j)),
            pl.BlockSpec((1, 1, tn), lambda l, j: (l, 0, j)),
        ],
        out_specs=pl.BlockSpec((1, COND_ROWS, tn), lambda l, j: (l, 0, j)),
        out_shape=jax.ShapeDtypeStruct((depth, COND_ROWS, 6 * D_MODEL), f32),
        compiler_params=pltpu.CompilerParams(vmem_limit_bytes=V7X_VMEM_LIMIT),
        name="ada",
    )(cond, ada_w, ada_b.reshape(depth, 1, 6 * D_MODEL))


def _mod_spec(layer, which, ctx):
    if ctx:
        return pl.BlockSpec((1, 1, D_MODEL), lambda b, i: ((layer * COND_ROWS + COND_ROWS // 2) * 6 + which, 0, 0))
    return pl.BlockSpec((1, 1, D_MODEL), lambda b, i: ((layer * COND_ROWS + b) * 6 + which, 0, 0))


def _rope_partner(x):
    lane = lax.broadcasted_iota(jnp.int32, x.shape, 1)
    first_half = (lane % 16) < 8
    n = x.shape[1]
    return jnp.where(first_half, pltpu.roll(x, n - 8, 1), pltpu.roll(x, 8, 1))


def _values_t(v):
    vt = v.T.astype(bf16).reshape(N_HEADS, HEAD_DIM, v.shape[0])
    return jnp.concatenate([vt, jnp.ones_like(vt)], axis=1)


def _proj_in_kernel(x_ref, sh_ref, sc_ref, g_ref, w_ref, bd_ref, cq_ref, sq_ref, ck_ref, sk_ref, lng_ref, lnb_ref,
                    acs_ref, naq_ref, nak_ref, nav_ref, sgu_ref, sgv_ref, dfq_ref, dfk_ref, dfv_ref):
    gw = GROUP_W
    gain = g_ref[...] * (1.0 + sc_ref[0])
    sub = min(x_ref.shape[1], PROJ_SUB_ROWS)
    for r0 in range(0, x_ref.shape[1], sub):
        rows = slice(r0, r0 + sub)
        x = x_ref[0, rows, :]
        r = lax.rsqrt(jnp.mean(x * x, axis=-1, keepdims=True) + EPS)
        hb = ((x * r) * gain + sh_ref[0]).astype(bf16)

        def mm(col, width=gw):
            return _dot(hb, w_ref[:, col:col + width])

        acs_ref[0, rows, :] = _dot(mm(0).astype(bf16), bd_ref[...]).astype(bf16)
        q = mm(6 * gw)
        dfq_ref[0, :, rows] = (q * cq_ref[rows, :] + _rope_partner(q) * sq_ref[rows, :]).T.astype(bf16)
        k = mm(7 * gw)
        dfk_ref[0, rows, :] = (k * ck_ref[rows, :] + _rope_partner(k) * sk_ref[rows, :]).astype(bf16)
        dfv_ref[0, :, :, rows] = _values_t(mm(8 * gw))
        uv = jax.nn.gelu(mm(4 * gw, 2 * gw), approximate=True)
        sgu_ref[0, rows, :] = uv[:, :gw].astype(bf16)
        v = uv[:, gw:]
        mu = jnp.mean(v, axis=-1, keepdims=True)
        vc = v - mu
        var = jnp.mean(vc * vc, axis=-1, keepdims=True)
        sgv_ref[0, rows, :] = (vc * lax.rsqrt(var + EPS) * lng_ref[...] + lnb_ref[...]).astype(bf16)
        nav_ref[0, :, :, rows] = _values_t(mm(3 * gw))
        naq_ref[0, :, rows] = (mm(gw) * NA_Q_SCALE).T.astype(bf16)
        nak_ref[0, rows, :] = mm(2 * gw).astype(bf16)


def _proj_in(x, mods, layer, ctx, g, w, bd, rope, lng, lnb):
    bsz, s, _ = x.shape
    tm = min(s, PROJ_ROWS)
    row = lambda b, i: (b, i, 0)
    const = lambda b, i: (0, 0)
    of_layer = lambda b, i: (layer, 0, 0)
    pos = lambda b, i: (i, 0)
    out = lambda width: pl.BlockSpec((1, tm, width), row)
    shape = lambda width: jax.ShapeDtypeStruct((bsz, s, width), bf16)
    vt = (pl.BlockSpec((1, N_HEADS, 2 * HEAD_DIM, tm), lambda b, i: (b, 0, 0, i)),
          jax.ShapeDtypeStruct((bsz, N_HEADS, 2 * HEAD_DIM, s), bf16))
    qt = (pl.BlockSpec((1, GROUP_W, tm), lambda b, i: (b, 0, i)), jax.ShapeDtypeStruct((bsz, GROUP_W, s), bf16))
    by_rows = lambda width: (out(width), shape(width))
    outs = (by_rows(2 * GROUP_W), qt, by_rows(GROUP_W), vt, by_rows(GROUP_W), by_rows(GROUP_W), qt,
            by_rows(GROUP_W), vt)
    return pl.pallas_call(
        _proj_in_kernel,
        grid=(bsz, s // tm),
        in_specs=[
            pl.BlockSpec((1, tm, D_MODEL), row),
            _mod_spec(layer, 0, ctx), _mod_spec(layer, 1, ctx),
            pl.BlockSpec((None, 1, D_MODEL), of_layer),
            pl.BlockSpec((None, D_MODEL, D_IN), of_layer),
            pl.BlockSpec((GROUP_W, 2 * GROUP_W), const),
            pl.BlockSpec((tm, GROUP_W), pos), pl.BlockSpec((tm, GROUP_W), pos),
            pl.BlockSpec((tm, GROUP_W), pos), pl.BlockSpec((tm, GROUP_W), pos),
            pl.BlockSpec((None, 1, GROUP_W), of_layer), pl.BlockSpec((None, 1, GROUP_W), of_layer),
        ],
        out_specs=[spec for spec, _ in outs],
        out_shape=[shp for _, shp in outs],
        compiler_params=pltpu.CompilerParams(vmem_limit_bytes=V7X_VMEM_LIMIT),
        name="proj_in_ctx" if ctx else "proj_in",
    )(x, mods, mods, g, w, bd, *rope, lng, lnb)


def _fourier_kernel(c_ref, s_ref, a_ref, o_ref, acc_ref):
    k = pl.program_id(1)
    gw = GROUP_W

    @pl.when(k == 0)
    def _():
        acc_ref[...] = jnp.zeros_like(acc_ref)

    for b in range(a_ref.shape[0]):
        acc_ref[b] += _dot(c_ref[...], a_ref[b, :, :gw]) + _dot(s_ref[...], a_ref[b, :, gw:])

    @pl.when(k == pl.num_programs(1) - 1)
    def _():
        o_ref[...] = acc_ref[...].astype(o_ref.dtype)


def _fourier(cn, sn, acs):
    bsz, s, _ = acs.shape
    tm = min(s, DFT_ROWS)
    tk = min(s, DFT_COLS)
    return pl.pallas_call(
        _fourier_kernel,
        grid=(s // tm, s // tk),
        in_specs=[
            pl.BlockSpec((tm, tk), lambda i, k: (i, k)),
            pl.BlockSpec((tm, tk), lambda i, k: (i, k)),
            pl.BlockSpec((bsz, tk, 2 * GROUP_W), lambda i, k: (0, k, 0)),
        ],
        out_specs=pl.BlockSpec((bsz, tm, GROUP_W), lambda i, k: (0, i, 0)),
        out_shape=jax.ShapeDtypeStruct((bsz, s, GROUP_W), bf16),
        scratch_shapes=[pltpu.VMEM((bsz, tm, GROUP_W), f32)],
        compiler_params=pltpu.CompilerParams(vmem_limit_bytes=V7X_VMEM_LIMIT),
        name="fourier",
    )(cn, sn, acs)


def _na_band_start(i):
    rows = GRID_W
    return jnp.clip(i * NA_Q_ROWS - NA_WIN_ROWS // 2, 0, rows - NA_BAND_ROWS)


def _na_bias_t(rpb, rows):
    depth = rpb.shape[0]
    cidx = np.arange(GRID_W)
    col_start = np.clip(cidx - NA_WIN_COLS // 2, 0, GRID_W - NA_WIN_COLS)
    in_win = (cidx[None, :] >= col_start[:, None]) & (cidx[None, :] < col_start[:, None] + NA_WIN_COLS)
    dc = np.clip(cidx[None, :] - cidx[:, None], 1 - NA_WIN_COLS, NA_WIN_COLS - 1) + NA_WIN_COLS - 1
    pick = (dc.T[None] == np.arange(2 * NA_WIN_COLS - 1)[:, None, None]).astype(np.float32)
    t = jnp.einsum('ldc,ckq->ldkq', rpb.reshape(depth * N_HEADS, *rpb.shape[2:]), pick,
                   precision=lax.Precision.HIGHEST)
    t = jnp.where(in_win.T[None, None], t * LOG2E, MASKED)
    outside = jnp.full_like(t[:, 0], MASKED)
    nblk = rows // NA_Q_ROWS
    cats = []
    for blk in (0, 1, nblk - 1):
        band0 = int(np.clip(blk * NA_Q_ROWS - NA_WIN_ROWS // 2, 0, rows - NA_BAND_ROWS))
        key_rows = []
        for kj in range(NA_BAND_ROWS):
            kr = band0 + kj
            per_q = []
            for qi in range(NA_Q_ROWS):
                r = blk * NA_Q_ROWS + qi
                r0 = int(np.clip(r - NA_WIN_ROWS // 2, 0, rows - NA_WIN_ROWS))
                per_q.append(t[:, kr - r + NA_WIN_ROWS - 1] if r0 <= kr < r0 + NA_WIN_ROWS else outside)
            key_rows.append(jnp.concatenate(per_q, axis=-1))
        cats.append(jnp.concatenate(key_rows, axis=1))
    return jnp.stack(cats).reshape(3 * depth, N_HEADS, NA_BAND_ROWS * GRID_W, NA_Q_ROWS * GRID_W)


def _sgu_kernel(u_ref, v_ref, w_ref, b_ref, o_ref):
    tm = u_ref.shape[1]
    lane = lax.broadcasted_iota(jnp.int32, (SGU_CHUNK, GROUP_W), 1)
    for c in range(tm // SGU_CHUNK):
        rows = slice(c * SGU_CHUNK, (c + 1) * SGU_CHUNK)
        vchunk = v_ref[0, rows, :]
        s = b_ref[...]
        for g in range(N_HEADS):
            in_group = (lane >= g * HEAD_DIM) & (lane < (g + 1) * HEAD_DIM)
            s = jnp.where(in_group, s + _dot(w_ref[g], vchunk), s)
        o_ref[0, rows, :] = (u_ref[0, rows, :].astype(f32) * s).astype(o_ref.dtype)


def _sgu(layer, u, v, w, b_full):
    bsz, s, _ = u.shape
    tm = min(s, ROW_TILE)
    row = pl.BlockSpec((1, tm, GROUP_W), lambda b, i: (b, i, 0))
    return pl.pallas_call(
        _sgu_kernel,
        grid=(bsz, s // tm),
        in_specs=[row, row,
                  pl.BlockSpec((None, N_HEADS, SGU_CHUNK, SGU_CHUNK), lambda b, i: (layer, 0, 0, 0)),
                  pl.BlockSpec((None, SGU_CHUNK, GROUP_W), lambda b, i: (layer, 0, 0))],
        out_specs=row,
        out_shape=jax.ShapeDtypeStruct((bsz, s, GROUP_W), bf16),
        name="sgu",
    )(u, v, w, b_full)


def _attn_t_kernel(*refs, layer, n_maps, post_scale, sources, n_blocks):
    lam_ref, q_ref = refs[:2]
    pos = 2
    i = pl.program_id(1)
    band_new = _na_band_start(jnp.minimum(i, n_blocks - 1)) * GRID_W
    band_old = _na_band_start(jnp.maximum(i - 1, 0)) * GRID_W
    chunks = []
    for n_keys, banded, biased in sources:
        k_ref, vt_ref = refs[pos:pos + 2]
        bias_ref = refs[pos + 2] if biased else None
        pos += 3 if biased else 2
        for off in range(0, n_keys, KEY_CHUNK):
            first = (pl.multiple_of(band_new + off, KEY_CHUNK), pl.multiple_of(band_old + off, KEY_CHUNK))
            chunks.append((k_ref, vt_ref, bias_ref) + (first if banded else (off, off)) + (off,))
    g_ref, o_ref, qm_ref, sa_ref, sb_ref, ma_ref, mb_ref, acc_ref = refs[pos:]

    n_streams = N_HEADS * n_maps
    width = GROUP_W // n_streams
    tq = q_ref.shape[2]
    qt = q_ref[0]
    chan = lax.broadcasted_iota(jnp.int32, qt.shape, 0)
    for j in range(n_streams):
        qm_ref[j] = jnp.where((chan >= j * width) & (chan < (j + 1) * width), qt, jnp.zeros_like(qt))

    @pl.when(i == 0)
    def _():
        sb_ref[...] = jnp.zeros_like(sb_ref)
        mb_ref[...] = jnp.zeros_like(mb_ref)
        acc_ref[...] = jnp.ones_like(acc_ref)

    def step(j_scores, j_values, to_first, values_of_old_block=False):
        s_new, m_new = (sa_ref, ma_ref) if to_first else (sb_ref, mb_ref)
        s_old, m_old = (sb_ref, mb_ref) if to_first else (sa_ref, ma_ref)
        mx = jnp.full((8, tq), MASKED, f32)
        m = m_old[...]
        acc = jnp.zeros((2 * HEAD_DIM, tq), f32)
        for c, (k_ref, vt_ref, bias_ref, first_new, first_old, bias_first) in enumerate(chunks):
            rows = slice(c * KEY_CHUNK, (c + 1) * KEY_CHUNK)
            s = _dot(k_ref[0, pl.ds(first_new, KEY_CHUNK), :], qm_ref[j_scores])
            if bias_ref is not None:
                s = s + bias_ref[j_scores // n_maps, bias_first:bias_first + KEY_CHUNK, :]
            s_new[rows, :] = s
            mx = jnp.maximum(mx, jnp.max(s.reshape(KEY_CHUNK // 8, 8, tq), axis=0))
            p = jnp.exp2(s_old[rows, :] - m).astype(bf16)
            first = first_old if values_of_old_block else first_new
            acc = acc + _dot(vt_ref[0, j_values // n_maps, :, pl.ds(first, KEY_CHUNK)], p)
        m_new[...] = jnp.max(mx, axis=0, keepdims=True)
        acc_ref[j_values] = acc

    step(0, n_streams - 1, True, values_of_old_block=True)
    _attn_finish(lam_ref, g_ref, acc_ref, o_ref, layer, n_maps, post_scale)

    @pl.when(i < n_blocks)
    def _():
        step(1, 0, False)

        def stream_pair(t, carry):
            step(2 * t + 2, 2 * t + 1, True)
            step(2 * t + 3, 2 * t + 2, False)
            return carry

        lax.fori_loop(0, n_streams // 2 - 1, stream_pair, 0)


def _attn_finish(lam_ref, g_ref, acc_ref, o_ref, layer, n_maps, post_scale):
    outs = []
    for h in range(N_HEADS):
        if n_maps == 1:
            a = acc_ref[h]
            o = a[:HEAD_DIM] / a[HEAD_DIM:HEAD_DIM + 1]
        else:
            a1 = acc_ref[2 * h]
            a2 = acc_ref[2 * h + 1]
            o = a1[:HEAD_DIM] / a1[HEAD_DIM:HEAD_DIM + 1] - lam_ref[layer] * (a2[:HEAD_DIM] / a2[HEAD_DIM:HEAD_DIM + 1])
            r = lax.rsqrt(jnp.mean(o * o, axis=0, keepdims=True) + EPS)
            o = (o * r) * g_ref[...] * post_scale
        outs.append(o)
    o_ref[0] = jnp.concatenate(outs, axis=0).T.astype(o_ref.dtype)


def _attn_t(name, layer, q, srcs, lam, g_rows, n_maps, post_scale):
    bsz, _, s = q.shape
    tq = min(s, ATTN_Q_BLOCK)
    nblk = s // tq
    n_streams = N_HEADS * n_maps
    n_total = sum(src[3] for src in srcs)
    in_specs = [
        pl.BlockSpec(memory_space=pltpu.SMEM),
        pl.BlockSpec((1, GROUP_W, tq), lambda b, i: (b, 0, jnp.minimum(i, nblk - 1))),
    ]
    args = [lam, q]
    for k, vt, bias, n_used, banded in srcs:
        in_specs += [
            pl.BlockSpec((1, k.shape[1], GROUP_W), lambda b, i: (b, 0, 0)),
            pl.BlockSpec((1, N_HEADS, 2 * HEAD_DIM, k.shape[1]), lambda b, i: (b, 0, 0, 0)),
        ]
        args += [k, vt]
        if bias is not None:
            n_layers = bias.shape[0] // 3
            in_specs.append(pl.BlockSpec(
                (None, N_HEADS, n_used, tq),
                lambda b, i: (jnp.where(i == 0, 0, jnp.where(i >= nblk - 1, 2, 1)) * n_layers + layer, 0, 0, 0)))
            args.append(bias)
    in_specs.append(pl.BlockSpec((None, HEAD_DIM, tq), lambda b, i: (layer, 0, 0)))
    args.append(g_rows)
    return pl.pallas_call(
        partial(_attn_t_kernel, layer=layer, n_maps=n_maps, post_scale=post_scale,
                sources=tuple((src[3], src[4], src[2] is not None) for src in srcs), n_blocks=nblk),
        grid=(bsz, nblk + 1),
        in_specs=in_specs,
        out_specs=pl.BlockSpec((1, tq, GROUP_W), lambda b, i: (b, jnp.maximum(i - 1, 0), 0)),
        out_shape=jax.ShapeDtypeStruct((bsz, s, GROUP_W), bf16),
        scratch_shapes=[
            pltpu.VMEM((n_streams, GROUP_W, tq), bf16),
            pltpu.VMEM((n_total, tq), f32), pltpu.VMEM((n_total, tq), f32),
            pltpu.VMEM((1, tq), f32), pltpu.VMEM((1, tq), f32),
            pltpu.VMEM((n_streams, 2 * HEAD_DIM, tq), f32),
        ],
        compiler_params=pltpu.CompilerParams(vmem_limit_bytes=V7X_VMEM_LIMIT),
        name=name,
    )(*args)


def _post_kernel(x_ref, ya_ref, yb_ref, yc_ref, yd_ref, g1_ref, sh_ref, sc_ref, g2_ref, ng_ref, wo_ref, w1_ref,
                 w2_ref, fg_ref, o_ref, *, final):
    y = jnp.concatenate([ya_ref[0], yb_ref[0], yc_ref[0], yd_ref[0]], axis=-1)
    x = x_ref[0] + g1_ref[0] * _dot(y, wo_ref[...])
    r = lax.rsqrt(jnp.mean(x * x, axis=-1, keepdims=True) + EPS)
    h = (x * r) * ng_ref[...]
    hb = (h * (1.0 + sc_ref[0]) + sh_ref[0]).astype(bf16)
    acc = jnp.zeros(x.shape, f32)
    step = D_MODEL
    for j in range(D_FF // step):
        t = jnp.maximum(_dot(hb, w1_ref[:, j * step:(j + 1) * step]), 0.0)
        acc = acc + _dot((t * t).astype(bf16), w2_ref[j * step:(j + 1) * step, :])
    x = x + g2_ref[0] * acc
    if final:
        r = lax.rsqrt(jnp.mean(x * x, axis=-1, keepdims=True) + EPS)
        x = (x * r) * fg_ref[...]
    o_ref[0] = x


def _post(x, ys, mods, layer, ctx, ng, wo, w1, w2, fg, final):
    bsz, s, _ = x.shape
    tm = min(s, ROW_TILE)
    row = lambda width: pl.BlockSpec((1, tm, width), lambda b, i: (b, i, 0))
    of_layer = lambda shape: pl.BlockSpec((None,) + shape, lambda b, i: (layer, 0, 0), pipeline_mode=pl.Buffered(1))
    return pl.pallas_call(
        partial(_post_kernel, final=final),
        grid=(bsz, s // tm),
        in_specs=[
            row(D_MODEL), row(GROUP_W), row(GROUP_W), row(GROUP_W), row(GROUP_W),
            _mod_spec(layer, 2, ctx), _mod_spec(layer, 3, ctx), _mod_spec(layer, 4, ctx), _mod_spec(layer, 5, ctx),
            of_layer((1, D_MODEL)), of_layer((D_MODEL, D_MODEL)), of_layer((D_MODEL, D_FF)),
            of_layer((D_FF, D_MODEL)),
            pl.BlockSpec((1, D_MODEL), lambda b, i: (0, 0)),
        ],
        out_specs=row(D_MODEL),
        out_shape=jax.ShapeDtypeStruct((bsz, s, D_MODEL), f32),
        compiler_params=pltpu.CompilerParams(vmem_limit_bytes=V7X_VMEM_LIMIT),
        name="post_ctx" if ctx else "post",
    )(x, *ys, mods, mods, mods, mods, ng, wo, w1, w2, fg)


def _channel_dft(n_pos):
    c = np.arange(HEAD_DIM)
    ang = 2.0 * np.pi * ((c[:, None] * c[None, :]) % HEAD_DIM) / HEAD_DIM
    scale = 1.0 / math.sqrt(n_pos * HEAD_DIM)
    bd = np.zeros((GROUP_W, 2 * GROUP_W), np.float32)
    for g in range(N_HEADS):
        sl = slice(g * HEAD_DIM, (g + 1) * HEAD_DIM)
        bd[sl, sl] = np.cos(ang) * scale
        bd[sl, GROUP_W + g * HEAD_DIM:GROUP_W + (g + 1) * HEAD_DIM] = -np.sin(ang) * scale
    return jnp.asarray(bd, bf16)


def _position_dft(n_pos):
    hi = max(n_pos // GRID_W, 1)
    lo = n_pos // hi
    col = jnp.arange(n_pos, dtype=jnp.int32)[None, :]
    a = 2.0 * np.pi * ((col * jnp.arange(hi, dtype=jnp.int32)[:, None] * lo) % n_pos).astype(f32) / n_pos
    b = 2.0 * np.pi * ((col * jnp.arange(lo, dtype=jnp.int32)[:, None]) % n_pos).astype(f32) / n_pos
    ca, sa, cb, sb = jnp.cos(a)[:, None, :], jnp.sin(a)[:, None, :], jnp.cos(b)[None, :, :], jnp.sin(b)[None, :, :]
    cn = (ca * cb - sa * sb).reshape(n_pos, n_pos)
    sn = (sa * cb + ca * sb).reshape(n_pos, n_pos)
    return cn.astype(bf16), sn.astype(bf16)


def _rope_tables(n):
    t = jnp.arange(n)
    rows = (t // GRID_W).astype(f32)
    cols = (t % GRID_W).astype(f32)
    n_freq = DIFF_QK_DIM // 4
    inv = ROPE_BASE ** (-jnp.arange(n_freq, dtype=f32) / n_freq)
    ang = jnp.concatenate([rows[:, None] * inv, cols[:, None] * inv], axis=-1)
    lane = np.arange(GROUP_W) % DIFF_QK_DIM
    src = (lane // 16) * n_freq + lane % n_freq
    sign = np.where((lane % 16) < n_freq, -1.0, 1.0).astype(np.float32)
    return jnp.cos(ang)[:, src], jnp.sin(ang)[:, src] * sign


def kernel(x, c, ctx, c_ctx, ada_w, ada_b, norm1_g, norm2_g, w_in, w_out, na_rpb, sgu_ln_g, sgu_ln_b, sgu_w, sgu_b,
           diff_lq1, diff_lk1, diff_lq2, diff_lk2, diff_subln_g, w_ff1, w_ff2, final_g):
    bsz, n, _ = x.shape
    n_ctx = ctx.shape[1]
    depth = w_in.shape[0]
    rows = n // GRID_W
    assert bsz <= COND_ROWS // 2

    cond = jnp.zeros((COND_ROWS, D_MODEL), f32).at[:bsz].set(c).at[COND_ROWS // 2].set(c_ctx)
    mods = _ada_mods(cond, ada_w, ada_b).reshape(depth * COND_ROWS * 6, 1, D_MODEL)

    cos_l, sin_l = _rope_tables(n)
    rope_lat = (cos_l * DIFF_Q_SCALE, sin_l * DIFF_Q_SCALE, cos_l, sin_l)
    ones = jnp.ones((n_ctx, GROUP_W), f32)
    rope_ctx = (ones * DIFF_Q_SCALE, ones * 0.0, ones, ones * 0.0)
    bd_lat, bd_ctx = _channel_dft(n), _channel_dft(n_ctx)
    dft_lat, dft_ctx = _position_dft(n), _position_dft(n_ctx)

    w_in_b, w_out_b = w_in.astype(bf16), w_out.astype(bf16)
    w1_b, w2_b, sgu_w_b = w_ff1.astype(bf16), w_ff2.astype(bf16), sgu_w.astype(bf16)
    fg = final_g.reshape(1, D_MODEL)
    g1 = norm1_g.reshape(depth, 1, D_MODEL)
    g2 = norm2_g.reshape(depth, 1, D_MODEL)
    lng = sgu_ln_g.reshape(depth, 1, GROUP_W)
    lnb = sgu_ln_b.reshape(depth, 1, GROUP_W)
    sgu_bias = jnp.repeat(sgu_b.transpose(0, 2, 1), HEAD_DIM, axis=2)
    subln = jnp.broadcast_to(diff_subln_g[:, :, None], (depth, HEAD_DIM, min(n, ATTN_Q_BLOCK)))
    lam_init = [0.8 - 0.6 * math.exp(-0.3 * l) for l in range(depth)]
    lam = (jnp.exp(jnp.sum(diff_lq1.astype(f32) * diff_lk1.astype(f32), axis=-1))
           - jnp.exp(jnp.sum(diff_lq2.astype(f32) * diff_lk2.astype(f32), axis=-1)) + jnp.asarray(lam_init, f32))
    na_bias = _na_bias_t(na_rpb, rows)
    band = NA_BAND_ROWS * GRID_W
    cx = ctx

    for l in range(depth):
        last = l == depth - 1
        post_scale = 1.0 - lam_init[l]

        p_lat = _proj_in(x, mods, l, False, g1, w_in_b, bd_lat, rope_lat, lng, lnb)
        p_ctx = _proj_in(cx, mods, l, True, g1, w_in_b, bd_ctx, rope_ctx, lng, lnb)
        acs, naq, nak, nav, sgu_u, sgu_v, dfq, dfk, dfv = p_lat
        c_acs, c_naq, c_nak, c_nav, c_sgu_u, c_sgu_v, c_dfq, c_dfk, c_dfv = p_ctx

        y_a = _fourier(*dft_lat, acs)
        ctx_na = (c_nak, c_nav, None, n_ctx, False)
        ctx_df = (c_dfk, c_dfv, None, n_ctx, False)
        y_b = _attn_t("na", l, naq, [(nak, nav, na_bias, band, True), ctx_na], lam, subln, 1, 1.0)
        y_c = _sgu(l, sgu_u, sgu_v, sgu_w_b, sgu_bias)
        y_d = _attn_t("diff", l, dfq, [(dfk, dfv, None, n, False), ctx_df], lam, subln, 2, post_scale)

        if not last:
            yc_a = _fourier(*dft_ctx, c_acs)
            yc_b = _attn_t("dense_ctx", l, c_naq, [ctx_na], lam, subln, 1, 1.0)
            yc_c = _sgu(l, c_sgu_u, c_sgu_v, sgu_w_b, sgu_bias)
            yc_d = _attn_t("diff_ctx", l, c_dfq, [ctx_df], lam, subln, 2, post_scale)
            cx = _post(cx, (yc_a, yc_b, yc_c, yc_d), mods, l, True, g2, w_out_b, w1_b, w2_b, fg, False)

        x = _post(x, (y_a, y_b, y_c, y_d), mods, l, False, g2, w_out_b, w1_b, w2_b, fg, last)
    return x
```

```python
import math
from functools import partial

import numpy as np
import jax
import jax.numpy as jnp
from jax import lax
from jax.experimental import pallas as pl
from jax.experimental.pallas import tpu as pltpu

D_MODEL = 1024
GRID_W = 64
N_HEADS = 4
GROUP_W = 256
HEAD_DIM = 64
NA_WIN_ROWS = 8
NA_WIN_COLS = 16
SGU_CHUNK = 128
DIFF_QK_DIM = 32
ROPE_BASE = 10000.0
D_FF = 4 * D_MODEL
D_IN = 9 * GROUP_W
EPS = 1e-6

LOG2E = 1.4426950408889634
NA_Q_SCALE = HEAD_DIM ** -0.5 * LOG2E
DIFF_Q_SCALE = DIFF_QK_DIM ** -0.5 * LOG2E
MASKED = -1e30
COND_ROWS = 16
KEY_CHUNK = 256
NA_Q_ROWS = 8
NA_BAND_ROWS = 16
ATTN_Q_BLOCK = NA_Q_ROWS * GRID_W
PROJ_ROWS = 1024
PROJ_SUB_ROWS = 256
ROW_TILE = 512
ADA_COLS = 1536
DFT_ROWS, DFT_COLS = 1024, 512
V7X_VMEM_LIMIT = 56 * 1024 * 1024

f32 = jnp.float32
bf16 = jnp.bfloat16


def _dot(a, b):
    return jnp.dot(a, b, preferred_element_type=f32)


def _ada_kernel(cond_ref, w_ref, b_ref, o_ref):
    c = cond_ref[...]
    s = (c * jax.nn.sigmoid(c)).astype(bf16)
    o_ref[0] = _dot(s, w_ref[0].astype(bf16)) + b_ref[0]


def _ada_mods(cond, ada_w, ada_b):
    depth = ada_w.shape[0]
    tn = ADA_COLS
    return pl.pallas_call(
        _ada_kernel,
        grid=(depth, 6 * D_MODEL // tn),
        in_specs=[
            pl.BlockSpec((COND_ROWS, D_MODEL), lambda l, j: (0, 0)),
            pl.BlockSpec((1, D_MODEL, tn), lambda l, j: (l, 0, j)),
            pl.BlockSpec((1, 1, tn), lambda l, j: (l, 0, j)),
        ],
        out_specs=pl.BlockSpec((1, COND_ROWS, tn), lambda l, j: (l, 0, j)),
        out_shape=jax.ShapeDtypeStruct((depth, COND_ROWS, 6 * D_MODEL), f32),
        compiler_params=pltpu.CompilerParams(vmem_limit_bytes=V7X_VMEM_LIMIT),
        name="ada",
    )(cond, ada_w, ada_b.reshape(depth, 1, 6 * D_MODEL))


def _mod_spec(layer, which, ctx):
    if ctx:
        return pl.BlockSpec((1, 1, D_MODEL), lambda b, i: ((layer * COND_ROWS + COND_ROWS // 2) * 6 + which, 0, 0))
    return pl.BlockSpec((1, 1, D_MODEL), lambda b, i: ((layer * COND_ROWS + b) * 6 + which, 0, 0))


def _rope_partner(x):
    lane = lax.broadcasted_iota(jnp.int32, x.shape, 1)
    first_half = (lane % 16) < 8
    n = x.shape[1]
    return jnp.where(first_half, pltpu.roll(x, n - 8, 1), pltpu.roll(x, 8, 1))


def _values_t(v):
    vt = v.T.astype(bf16).reshape(N_HEADS, HEAD_DIM, v.shape[0])
    return jnp.concatenate([vt, jnp.ones_like(vt)], axis=1)


def _proj_in_kernel(x_ref, sh_ref, sc_ref, g_ref, w_ref, bd_ref, cq_ref, sq_ref, ck_ref, sk_ref, lng_ref, lnb_ref,
                    acs_ref, naq_ref, nak_ref, nav_ref, sgu_ref, sgv_ref, dfq_ref, dfk_ref, dfv_ref):
    gw = GROUP_W
    gain = g_ref[...] * (1.0 + sc_ref[0])
    sub = min(x_ref.shape[1], PROJ_SUB_ROWS)
    for r0 in range(0, x_ref.shape[1], sub):
        rows = slice(r0, r0 + sub)
        x = x_ref[0, rows, :]
        r = lax.rsqrt(jnp.mean(x * x, axis=-1, keepdims=True) + EPS)
        hb = ((x * r) * gain + sh_ref[0]).astype(bf16)

        def mm(col, width=gw):
            return _dot(hb, w_ref[:, col:col + width])

        acs_ref[0, rows, :] = _dot(mm(0).astype(bf16), bd_ref[...]).astype(bf16)
        q = mm(6 * gw)
        dfq_ref[0, :, rows] = (q * cq_ref[rows, :] + _rope_partner(q) * sq_ref[rows, :]).T.astype(bf16)
        k = mm(7 * gw)
        dfk_ref[0, rows, :] = (k * ck_ref[rows, :] + _rope_partner(k) * sk_ref[rows, :]).astype(bf16)
        dfv_ref[0, :, :, rows] = _values_t(mm(8 * gw))
        uv = jax.nn.gelu(mm(4 * gw, 2 * gw), approximate=True)
        sgu_ref[0, rows, :] = uv[:, :gw].astype(bf16)
        v = uv[:, gw:]
        mu = jnp.mean(v, axis=-1, keepdims=True)
        vc = v - mu
        var = jnp.mean(vc * vc, axis=-1, keepdims=True)
        sgv_ref[0, rows, :] = (vc * lax.rsqrt(var + EPS) * lng_ref[...] + lnb_ref[...]).astype(bf16)
        nav_ref[0, :, :, rows] = _values_t(mm(3 * gw))
        naq_ref[0, :, rows] = (mm(gw) * NA_Q_SCALE).T.astype(bf16)
        nak_ref[0, rows, :] = mm(2 * gw).astype(bf16)


def _proj_in(x, mods, layer, ctx, g, w, bd, rope, lng, lnb):
    bsz, s, _ = x.shape
    tm = min(s, PROJ_ROWS)
    row = lambda b, i: (b, i, 0)
    const = lambda b, i: (0, 0)
    of_layer = lambda b, i: (layer, 0, 0)
    pos = lambda b, i: (i, 0)
    out = lambda width: pl.BlockSpec((1, tm, width), row)
    shape = lambda width: jax.ShapeDtypeStruct((bsz, s, width), bf16)
    vt = (pl.BlockSpec((1, N_HEADS, 2 * HEAD_DIM, tm), lambda b, i: (b, 0, 0, i)),
          jax.ShapeDtypeStruct((bsz, N_HEADS, 2 * HEAD_DIM, s), bf16))
    qt = (pl.BlockSpec((1, GROUP_W, tm), lambda b, i: (b, 0, i)), jax.ShapeDtypeStruct((bsz, GROUP_W, s), bf16))
    by_rows = lambda width: (out(width), shape(width))
    outs = (by_rows(2 * GROUP_W), qt, by_rows(GROUP_W), vt, by_rows(GROUP_W), by_rows(GROUP_W), qt,
            by_rows(GROUP_W), vt)
    return pl.pallas_call(
        _proj_in_kernel,
        grid=(bsz, s // tm),
        in_specs=[
            pl.BlockSpec((1, tm, D_MODEL), row),
            _mod_spec(layer, 0, ctx), _mod_spec(layer, 1, ctx),
            pl.BlockSpec((None, 1, D_MODEL), of_layer),
            pl.BlockSpec((None, D_MODEL, D_IN), of_layer),
            pl.BlockSpec((GROUP_W, 2 * GROUP_W), const),
            pl.BlockSpec((tm, GROUP_W), pos), pl.BlockSpec((tm, GROUP_W), pos),
            pl.BlockSpec((tm, GROUP_W), pos), pl.BlockSpec((tm, GROUP_W), pos),
            pl.BlockSpec((None, 1, GROUP_W), of_layer), pl.BlockSpec((None, 1, GROUP_W), of_layer),
        ],
        out_specs=[spec for spec, _ in outs],
        out_shape=[shp for _, shp in outs],
        compiler_params=pltpu.CompilerParams(vmem_limit_bytes=V7X_VMEM_LIMIT),
        name="proj_in_ctx" if ctx else "proj_in",
    )(x, mods, mods, g, w, bd, *rope, lng, lnb)


def _fourier_kernel(c_ref, s_ref, a_ref, o_ref, acc_ref):
    k = pl.program_id(1)
    gw = GROUP_W

    @pl.when(k == 0)
    def _():
        acc_ref[...] = jnp.zeros_like(acc_ref)

    for b in range(a_ref.shape[0]):
        acc_ref[b] += _dot(c_ref[...], a_ref[b, :, :gw]) + _dot(s_ref[...], a_ref[b, :, gw:])

    @pl.when(k == pl.num_programs(1) - 1)
    def _():
        o_ref[...] = acc_ref[...].astype(o_ref.dtype)


def _fourier(cn, sn, acs):
    bsz, s, _ = acs.shape
    tm = min(s, DFT_ROWS)
    tk = min(s, DFT_COLS)
    return pl.pallas_call(
        _fourier_kernel,
        grid=(s // tm, s // tk),
        in_specs=[
            pl.BlockSpec((tm, tk), lambda i, k: (i, k)),
            pl.BlockSpec((tm, tk), lambda i, k: (i, k)),
            pl.BlockSpec((bsz, tk, 2 * GROUP_W), lambda i, k: (0, k, 0)),
        ],
        out_specs=pl.BlockSpec((bsz, tm, GROUP_W), lambda i, k: (0, i, 0)),
        out_shape=jax.ShapeDtypeStruct((bsz, s, GROUP_W), bf16),
        scratch_shapes=[pltpu.VMEM((bsz, tm, GROUP_W), f32)],
        compiler_params=pltpu.CompilerParams(vmem_limit_bytes=V7X_VMEM_LIMIT),
        name="fourier",
    )(cn, sn, acs)


def _na_band_start(i):
    rows = GRID_W
    return jnp.clip(i * NA_Q_ROWS - NA_WIN_ROWS // 2, 0, rows - NA_BAND_ROWS)


def _na_bias_t(rpb, rows):
    depth = rpb.shape[0]
    cidx = np.arange(GRID_W)
    col_start = np.clip(cidx - NA_WIN_COLS // 2, 0, GRID_W - NA_WIN_COLS)
    in_win = (cidx[None, :] >= col_start[:, None]) & (cidx[None, :] < col_start[:, None] + NA_WIN_COLS)
    dc = np.clip(cidx[None, :] - cidx[:, None], 1 - NA_WIN_COLS, NA_WIN_COLS - 1) + NA_WIN_COLS - 1
    pick = (dc.T[None] == np.arange(2 * NA_WIN_COLS - 1)[:, None, None]).astype(np.float32)
    t = jnp.einsum('ldc,ckq->ldkq', rpb.reshape(depth * N_HEADS, *rpb.shape[2:]), pick,
                   precision=lax.Precision.HIGHEST)
    t = jnp.where(in_win.T[None, None], t * LOG2E, MASKED)
    outside = jnp.full_like(t[:, 0], MASKED)
    nblk = rows // NA_Q_ROWS
    cats = []
    for blk in (0, 1, nblk - 1):
        band0 = int(np.clip(blk * NA_Q_ROWS - NA_WIN_ROWS // 2, 0, rows - NA_BAND_ROWS))
        key_rows = []
        for kj in range(NA_BAND_ROWS):
            kr = band0 + kj
            per_q = []
            for qi in range(NA_Q_ROWS):
                r = blk * NA_Q_ROWS + qi
                r0 = int(np.clip(r - NA_WIN_ROWS // 2, 0, rows - NA_WIN_ROWS))
                per_q.append(t[:, kr - r + NA_WIN_ROWS - 1] if r0 <= kr < r0 + NA_WIN_ROWS else outside)
            key_rows.append(jnp.concatenate(per_q, axis=-1))
        cats.append(jnp.concatenate(key_rows, axis=1))
    return jnp.stack(cats).reshape(3 * depth, N_HEADS, NA_BAND_ROWS * GRID_W, NA_Q_ROWS * GRID_W)


def _sgu_kernel(u_ref, v_ref, w_ref, b_ref, o_ref):
    tm = u_ref.shape[1]
    lane = lax.broadcasted_iota(jnp.int32, (SGU_CHUNK, GROUP_W), 1)
    for c in range(tm // SGU_CHUNK):
        rows = slice(c * SGU_CHUNK, (c + 1) * SGU_CHUNK)
        vchunk = v_ref[0, rows, :]
        s = b_ref[...]
        for g in range(N_HEADS):
            in_group = (lane >= g * HEAD_DIM) & (lane < (g + 1) * HEAD_DIM)
            s = jnp.where(in_group, s + _dot(w_ref[g], vchunk), s)
        o_ref[0, rows, :] = (u_ref[0, rows, :].astype(f32) * s).astype(o_ref.dtype)


def _sgu(layer, u, v, w, b_full):
    bsz, s, _ = u.shape
    tm = min(s, ROW_TILE)
    row = pl.BlockSpec((1, tm, GROUP_W), lambda b, i: (b, i, 0))
    return pl.pallas_call(
        _sgu_kernel,
        grid=(bsz, s // tm),
        in_specs=[row, row,
                  pl.BlockSpec((None, N_HEADS, SGU_CHUNK, SGU_CHUNK), lambda b, i: (layer, 0, 0, 0)),
                  pl.BlockSpec((None, SGU_CHUNK, GROUP_W), lambda b, i: (layer, 0, 0))],
        out_specs=row,
        out_shape=jax.ShapeDtypeStruct((bsz, s, GROUP_W), bf16),
        name="sgu",
    )(u, v, w, b_full)


def _attn_t_kernel(*refs, layer, n_maps, post_scale, sources, n_blocks):
    lam_ref, q_ref = refs[:2]
    pos = 2
    i = pl.program_id(1)
    band_new = _na_band_start(jnp.minimum(i, n_blocks - 1)) * GRID_W
    band_old = _na_band_start(jnp.maximum(i - 1, 0)) * GRID_W
    chunks = []
    for n_keys, banded, biased in sources:
        k_ref, vt_ref = refs[pos:pos + 2]
        bias_ref = refs[pos + 2] if biased else None
        pos += 3 if biased else 2
        for off in range(0, n_keys, KEY_CHUNK):
            first = (pl.multiple_of(band_new + off, KEY_CHUNK), pl.multiple_of(band_old + off, KEY_CHUNK))
            chunks.append((k_ref, vt_ref, bias_ref) + (first if banded else (off, off)) + (off,))
    g_ref, o_ref, qm_ref, sa_ref, sb_ref, ma_ref, mb_ref, acc_ref = refs[pos:]

    n_streams = N_HEADS * n_maps
    width = GROUP_W // n_streams
    tq = q_ref.shape[2]
    qt = q_ref[0]
    chan = lax.broadcasted_iota(jnp.int32, qt.shape, 0)
    for j in range(n_streams):
        qm_ref[j] = jnp.where((chan >= j * width) & (chan < (j + 1) * width), qt, jnp.zeros_like(qt))

    @pl.when(i == 0)
    def _():
        sb_ref[...] = jnp.zeros_like(sb_ref)
        mb_ref[...] = jnp.zeros_like(mb_ref)
        acc_ref[...] = jnp.ones_like(acc_ref)

    def step(j_scores, j_values, to_first, values_of_old_block=False):
        s_new, m_new = (sa_ref, ma_ref) if to_first else (sb_ref, mb_ref)
        s_old, m_old = (sb_ref, mb_ref) if to_first else (sa_ref, ma_ref)
        mx = jnp.full((8, tq), MASKED, f32)
        m = m_old[...]
        acc = jnp.zeros((2 * HEAD_DIM, tq), f32)
        for c, (k_ref, vt_ref, bias_ref, first_new, first_old, bias_first) in enumerate(chunks):
            rows = slice(c * KEY_CHUNK, (c + 1) * KEY_CHUNK)
            s = _dot(k_ref[0, pl.ds(first_new, KEY_CHUNK), :], qm_ref[j_scores])
            if bias_ref is not None:
                s = s + bias_ref[j_scores // n_maps, bias_first:bias_first + KEY_CHUNK, :]
            s_new[rows, :] = s
            mx = jnp.maximum(mx, jnp.max(s.reshape(KEY_CHUNK // 8, 8, tq), axis=0))
            p = jnp.exp2(s_old[rows, :] - m).astype(bf16)
            first = first_old if values_of_old_block else first_new
            acc = acc + _dot(vt_ref[0, j_values // n_maps, :, pl.ds(first, KEY_CHUNK)], p)
        m_new[...] = jnp.max(mx, axis=0, keepdims=True)
        acc_ref[j_values] = acc

    step(0, n_streams - 1, True, values_of_old_block=True)
    _attn_finish(lam_ref, g_ref, acc_ref, o_ref, layer, n_maps, post_scale)

    @pl.when(i < n_blocks)
    def _():
        for j in range(1, n_streams):
            step(j, j - 1, j % 2 == 0)


def _attn_finish(lam_ref, g_ref, acc_ref, o_ref, layer, n_maps, post_scale):
    outs = []
    for h in range(N_HEADS):
        if n_maps == 1:
            a = acc_ref[h]
            o = a[:HEAD_DIM] / a[HEAD_DIM:HEAD_DIM + 1]
        else:
            a1 = acc_ref[2 * h]
            a2 = acc_ref[2 * h + 1]
            o = a1[:HEAD_DIM] / a1[HEAD_DIM:HEAD_DIM + 1] - lam_ref[layer] * (a2[:HEAD_DIM] / a2[HEAD_DIM:HEAD_DIM + 1])
            r = lax.rsqrt(jnp.mean(o * o, axis=0, keepdims=True) + EPS)
            o = (o * r) * g_ref[...] * post_scale
        outs.append(o)
    o_ref[0] = jnp.concatenate(outs, axis=0).T.astype(o_ref.dtype)


def _attn_t(name, layer, q, srcs, lam, g_rows, n_maps, post_scale):
    bsz, _, s = q.shape
    tq = min(s, ATTN_Q_BLOCK)
    nblk = s // tq
    n_streams = N_HEADS * n_maps
    n_total = sum(src[3] for src in srcs)
    in_specs = [
        pl.BlockSpec(memory_space=pltpu.SMEM),
        pl.BlockSpec((1, GROUP_W, tq), lambda b, i: (b, 0, jnp.minimum(i, nblk - 1))),
    ]
    args = [lam, q]
    for k, vt, bias, n_used, banded in srcs:
        in_specs += [
            pl.BlockSpec((1, k.shape[1], GROUP_W), lambda b, i: (b, 0, 0)),
            pl.BlockSpec((1, N_HEADS, 2 * HEAD_DIM, k.shape[1]), lambda b, i: (b, 0, 0, 0)),
        ]
        args += [k, vt]
        if bias is not None:
            n_layers = bias.shape[0] // 3
            in_specs.append(pl.BlockSpec(
                (None, N_HEADS, n_used, tq),
                lambda b, i: (jnp.where(i == 0, 0, jnp.where(i >= nblk - 1, 2, 1)) * n_layers + layer, 0, 0, 0)))
            args.append(bias)
    in_specs.append(pl.BlockSpec((None, HEAD_DIM, tq), lambda b, i: (layer, 0, 0)))
    args.append(g_rows)
    return pl.pallas_call(
        partial(_attn_t_kernel, layer=layer, n_maps=n_maps, post_scale=post_scale,
                sources=tuple((src[3], src[4], src[2] is not None) for src in srcs), n_blocks=nblk),
        grid=(bsz, nblk + 1),
        in_specs=in_specs,
        out_specs=pl.BlockSpec((1, tq, GROUP_W), lambda b, i: (b, jnp.maximum(i - 1, 0), 0)),
        out_shape=jax.ShapeDtypeStruct((bsz, s, GROUP_W), bf16),
        scratch_shapes=[
            pltpu.VMEM((n_streams, GROUP_W, tq), bf16),
            pltpu.VMEM((n_total, tq), f32), pltpu.VMEM((n_total, tq), f32),
            pltpu.VMEM((1, tq), f32), pltpu.VMEM((1, tq), f32),
            pltpu.VMEM((n_streams, 2 * HEAD_DIM, tq), f32),
        ],
        compiler_params=pltpu.CompilerParams(vmem_limit_bytes=V7X_VMEM_LIMIT),
        name=name,
    )(*args)


def _post_kernel(x_ref, ya_ref, yb_ref, yc_ref, yd_ref, g1_ref, sh_ref, sc_ref, g2_ref, ng_ref, wo_ref, w1_ref,
                 w2_ref, fg_ref, o_ref, *, final):
    y = jnp.concatenate([ya_ref[0], yb_ref[0], yc_ref[0], yd_ref[0]], axis=-1)
    x = x_ref[0] + g1_ref[0] * _dot(y, wo_ref[...])
    r = lax.rsqrt(jnp.mean(x * x, axis=-1, keepdims=True) + EPS)
    h = (x * r) * ng_ref[...]
    hb = (h * (1.0 + sc_ref[0]) + sh_ref[0]).astype(bf16)
    acc = jnp.zeros(x.shape, f32)
    step = D_MODEL
    for j in range(D_FF // step):
        t = jnp.maximum(_dot(hb, w1_ref[:, j * step:(j + 1) * step]), 0.0)
        acc = acc + _dot((t * t).astype(bf16), w2_ref[j * step:(j + 1) * step, :])
    x = x + g2_ref[0] * acc
    if final:
        r = lax.rsqrt(jnp.mean(x * x, axis=-1, keepdims=True) + EPS)
        x = (x * r) * fg_ref[...]
    o_ref[0] = x


def _post(x, ys, mods, layer, ctx, ng, wo, w1, w2, fg, final):
    bsz, s, _ = x.shape
    tm = min(s, ROW_TILE)
    row = lambda width: pl.BlockSpec((1, tm, width), lambda b, i: (b, i, 0))
    of_layer = lambda shape: pl.BlockSpec((None,) + shape, lambda b, i: (layer, 0, 0), pipeline_mode=pl.Buffered(1))
    return pl.pallas_call(
        partial(_post_kernel, final=final),
        grid=(bsz, s // tm),
        in_specs=[
            row(D_MODEL), row(GROUP_W), row(GROUP_W), row(GROUP_W), row(GROUP_W),
            _mod_spec(layer, 2, ctx), _mod_spec(layer, 3, ctx), _mod_spec(layer, 4, ctx), _mod_spec(layer, 5, ctx),
            of_layer((1, D_MODEL)), of_layer((D_MODEL, D_MODEL)), of_layer((D_MODEL, D_FF)),
            of_layer((D_FF, D_MODEL)),
            pl.BlockSpec((1, D_MODEL), lambda b, i: (0, 0)),
        ],
        out_specs=row(D_MODEL),
        out_shape=jax.ShapeDtypeStruct((bsz, s, D_MODEL), f32),
        compiler_params=pltpu.CompilerParams(vmem_limit_bytes=V7X_VMEM_LIMIT),
        name="post_ctx" if ctx else "post",
    )(x, *ys, mods, mods, mods, mods, ng, wo, w1, w2, fg)


def _channel_dft(n_pos):
    c = np.arange(HEAD_DIM)
    ang = 2.0 * np.pi * ((c[:, None] * c[None, :]) % HEAD_DIM) / HEAD_DIM
    scale = 1.0 / math.sqrt(n_pos * HEAD_DIM)
    bd = np.zeros((GROUP_W, 2 * GROUP_W), np.float32)
    for g in range(N_HEADS):
        sl = slice(g * HEAD_DIM, (g + 1) * HEAD_DIM)
        bd[sl, sl] = np.cos(ang) * scale
        bd[sl, GROUP_W + g * HEAD_DIM:GROUP_W + (g + 1) * HEAD_DIM] = -np.sin(ang) * scale
    return jnp.asarray(bd, bf16)


def _position_dft(n_pos):
    hi = max(n_pos // GRID_W, 1)
    lo = n_pos // hi
    col = jnp.arange(n_pos, dtype=jnp.int32)[None, :]
    a = 2.0 * np.pi * ((col * jnp.arange(hi, dtype=jnp.int32)[:, None] * lo) % n_pos).astype(f32) / n_pos
    b = 2.0 * np.pi * ((col * jnp.arange(lo, dtype=jnp.int32)[:, None]) % n_pos).astype(f32) / n_pos
    ca, sa, cb, sb = jnp.cos(a)[:, None, :], jnp.sin(a)[:, None, :], jnp.cos(b)[None, :, :], jnp.sin(b)[None, :, :]
    cn = (ca * cb - sa * sb).reshape(n_pos, n_pos)
    sn = (sa * cb + ca * sb).reshape(n_pos, n_pos)
    return cn.astype(bf16), sn.astype(bf16)


def _rope_tables(n):
    t = jnp.arange(n)
    rows = (t // GRID_W).astype(f32)
    cols = (t % GRID_W).astype(f32)
    n_freq = DIFF_QK_DIM // 4
    inv = ROPE_BASE ** (-jnp.arange(n_freq, dtype=f32) / n_freq)
    ang = jnp.concatenate([rows[:, None] * inv, cols[:, None] * inv], axis=-1)
    lane = np.arange(GROUP_W) % DIFF_QK_DIM
    src = (lane // 16) * n_freq + lane % n_freq
    sign = np.where((lane % 16) < n_freq, -1.0, 1.0).astype(np.float32)
    return jnp.cos(ang)[:, src], jnp.sin(ang)[:, src] * sign


def kernel(x, c, ctx, c_ctx, ada_w, ada_b, norm1_g, norm2_g, w_in, w_out, na_rpb, sgu_ln_g, sgu_ln_b, sgu_w, sgu_b,
           diff_lq1, diff_lk1, diff_lq2, diff_lk2, diff_subln_g, w_ff1, w_ff2, final_g):
    bsz, n, _ = x.shape
    n_ctx = ctx.shape[1]
    depth = w_in.shape[0]
    rows = n // GRID_W
    assert bsz <= COND_ROWS // 2

    cond = jnp.zeros((COND_ROWS, D_MODEL), f32).at[:bsz].set(c).at[COND_ROWS // 2].set(c_ctx)
    mods = _ada_mods(cond, ada_w, ada_b).reshape(depth * COND_ROWS * 6, 1, D_MODEL)

    cos_l, sin_l = _rope_tables(n)
    rope_lat = (cos_l * DIFF_Q_SCALE, sin_l * DIFF_Q_SCALE, cos_l, sin_l)
    ones = jnp.ones((n_ctx, GROUP_W), f32)
    rope_ctx = (ones * DIFF_Q_SCALE, ones * 0.0, ones, ones * 0.0)
    bd_lat, bd_ctx = _channel_dft(n), _channel_dft(n_ctx)
    dft_lat, dft_ctx = _position_dft(n), _position_dft(n_ctx)

    w_in_b, w_out_b = w_in.astype(bf16), w_out.astype(bf16)
    w1_b, w2_b, sgu_w_b = w_ff1.astype(bf16), w_ff2.astype(bf16), sgu_w.astype(bf16)
    fg = final_g.reshape(1, D_MODEL)
    g1 = norm1_g.reshape(depth, 1, D_MODEL)
    g2 = norm2_g.reshape(depth, 1, D_MODEL)
    lng = sgu_ln_g.reshape(depth, 1, GROUP_W)
    lnb = sgu_ln_b.reshape(depth, 1, GROUP_W)
    sgu_bias = jnp.repeat(sgu_b.transpose(0, 2, 1), HEAD_DIM, axis=2)
    subln = jnp.broadcast_to(diff_subln_g[:, :, None], (depth, HEAD_DIM, min(n, ATTN_Q_BLOCK)))
    lam_init = [0.8 - 0.6 * math.exp(-0.3 * l) for l in range(depth)]
    lam = (jnp.exp(jnp.sum(diff_lq1.astype(f32) * diff_lk1.astype(f32), axis=-1))
           - jnp.exp(jnp.sum(diff_lq2.astype(f32) * diff_lk2.astype(f32), axis=-1)) + jnp.asarray(lam_init, f32))
    na_bias = _na_bias_t(na_rpb, rows)
    band = NA_BAND_ROWS * GRID_W
    cx = ctx

    for l in range(depth):
        last = l == depth - 1
        post_scale = 1.0 - lam_init[l]

        p_lat = _proj_in(x, mods, l, False, g1, w_in_b, bd_lat, rope_lat, lng, lnb)
        p_ctx = _proj_in(cx, mods, l, True, g1, w_in_b, bd_ctx, rope_ctx, lng, lnb)
        acs, naq, nak, nav, sgu_u, sgu_v, dfq, dfk, dfv = p_lat
        c_acs, c_naq, c_nak, c_nav, c_sgu_u, c_sgu_v, c_dfq, c_dfk, c_dfv = p_ctx

        y_a = _fourier(*dft_lat, acs)
        ctx_na = (c_nak, c_nav, None, n_ctx, False)
        ctx_df = (c_dfk, c_dfv, None, n_ctx, False)
        y_b = _attn_t("na", l, naq, [(nak, nav, na_bias, band, True), ctx_na], lam, subln, 1, 1.0)
        y_c = _sgu(l, sgu_u, sgu_v, sgu_w_b, sgu_bias)
        y_d = _attn_t("diff", l, dfq, [(dfk, dfv, None, n, False), ctx_df], lam, subln, 2, post_scale)

        if not last:
            yc_a = _fourier(*dft_ctx, c_acs)
            yc_b = _attn_t("dense_ctx", l, c_naq, [ctx_na], lam, subln, 1, 1.0)
            yc_c = _sgu(l, c_sgu_u, c_sgu_v, sgu_w_b, sgu_bias)
            yc_d = _attn_t("diff_ctx", l, c_dfq, [ctx_df], lam, subln, 2, post_scale)
            cx = _post(cx, (yc_a, yc_b, yc_c, yc_d), mods, l, True, g2, w_out_b, w1_b, w2_b, fg, False)

        x = _post(x, (y_a, y_b, y_c, y_d), mods, l, False, g2, w_out_b, w1_b, w2_b, fg, last)
    return x
```

```python
import math
from functools import partial

import numpy as np
import jax
import jax.numpy as jnp
from jax import lax
from jax.experimental import pallas as pl
from jax.experimental.pallas import tpu as pltpu

D_MODEL = 1024
GRID_W = 64
N_HEADS = 4
GROUP_W = 256
HEAD_DIM = 64
NA_WIN_ROWS = 8
NA_WIN_COLS = 16
SGU_CHUNK = 128
DIFF_QK_DIM = 32
ROPE_BASE = 10000.0
D_FF = 4 * D_MODEL
D_IN = 9 * GROUP_W
EPS = 1e-6

LOG2E = 1.4426950408889634
NA_Q_SCALE = HEAD_DIM ** -0.5 * LOG2E
DIFF_Q_SCALE = DIFF_QK_DIM ** -0.5 * LOG2E
MASKED = -1e30
COND_ROWS = 16
KEY_CHUNK = 256
NA_Q_ROWS = 8
NA_BAND_ROWS = 16
ATTN_Q_BLOCK = NA_Q_ROWS * GRID_W
PROJ_ROWS = 1024
PROJ_SUB_ROWS = 256
ROW_TILE = 512
ADA_COLS = 1536
DFT_ROWS, DFT_COLS = 1024, 512
V7X_VMEM_LIMIT = 56 * 1024 * 1024

f32 = jnp.float32
bf16 = jnp.bfloat16


def _dot(a, b):
    return jnp.dot(a, b, preferred_element_type=f32)


def _ada_kernel(cond_ref, w_ref, b_ref, o_ref):
    c = cond_ref[...]
    s = (c * jax.nn.sigmoid(c)).astype(bf16)
    o_ref[0] = _dot(s, w_ref[0].astype(bf16)) + b_ref[0]


def _ada_mods(cond, ada_w, ada_b):
    depth = ada_w.shape[0]
    tn = ADA_COLS
    return pl.pallas_call(
        _ada_kernel,
        grid=(depth, 6 * D_MODEL // tn),
        in_specs=[
            pl.BlockSpec((COND_ROWS, D_MODEL), lambda l, j: (0, 0)),
            pl.BlockSpec((1, D_MODEL, tn), lambda l, j: (l, 0, j)),
            pl.BlockSpec((1, 1, tn), lambda l, j: (l, 0, j)),
        ],
        out_specs=pl.BlockSpec((1, COND_ROWS, tn), lambda l, j: (l, 0, j)),
        out_shape=jax.ShapeDtypeStruct((depth, COND_ROWS, 6 * D_MODEL), f32),
        compiler_params=pltpu.CompilerParams(vmem_limit_bytes=V7X_VMEM_LIMIT),
        name="ada",
    )(cond, ada_w, ada_b.reshape(depth, 1, 6 * D_MODEL))


def _mod_spec(layer, which, ctx):
    if ctx:
        return pl.BlockSpec((1, 1, D_MODEL), lambda b, i: ((layer * COND_ROWS + COND_ROWS // 2) * 6 + which, 0, 0))
    return pl.BlockSpec((1, 1, D_MODEL), lambda b, i: ((layer * COND_ROWS + b) * 6 + which, 0, 0))


def _rope_partner(x):
    lane = lax.broadcasted_iota(jnp.int32, x.shape, 1)
    first_half = (lane % 16) < 8
    n = x.shape[1]
    return jnp.where(first_half, pltpu.roll(x, n - 8, 1), pltpu.roll(x, 8, 1))


def _values_t(v):
    vt = v.T.astype(bf16).reshape(N_HEADS, HEAD_DIM, v.shape[0])
    return jnp.concatenate([vt, jnp.ones_like(vt)], axis=1)


def _proj_in_kernel(x_ref, sh_ref, sc_ref, g_ref, w_ref, bd_ref, cq_ref, sq_ref, ck_ref, sk_ref, lng_ref, lnb_ref,
                    acs_ref, naq_ref, nak_ref, nav_ref, sgu_ref, sgv_ref, dfq_ref, dfk_ref, dfv_ref):
    gw = GROUP_W
    gain = g_ref[...] * (1.0 + sc_ref[0])
    sub = min(x_ref.shape[1], PROJ_SUB_ROWS)
    for r0 in range(0, x_ref.shape[1], sub):
        rows = slice(r0, r0 + sub)
        x = x_ref[0, rows, :]
        r = lax.rsqrt(jnp.mean(x * x, axis=-1, keepdims=True) + EPS)
        hb = ((x * r) * gain + sh_ref[0]).astype(bf16)

        def mm(col, width=gw):
            return _dot(hb, w_ref[:, col:col + width])

        acs_ref[0, rows, :] = _dot(mm(0).astype(bf16), bd_ref[...]).astype(bf16)
        q = mm(6 * gw)
        dfq_ref[0, :, rows] = (q * cq_ref[rows, :] + _rope_partner(q) * sq_ref[rows, :]).T.astype(bf16)
        k = mm(7 * gw)
        dfk_ref[0, rows, :] = (k * ck_ref[rows, :] + _rope_partner(k) * sk_ref[rows, :]).astype(bf16)
        dfv_ref[0, :, :, rows] = _values_t(mm(8 * gw))
        uv = jax.nn.gelu(mm(4 * gw, 2 * gw), approximate=True)
        sgu_ref[0, rows, :] = uv[:, :gw].astype(bf16)
        v = uv[:, gw:]
        mu = jnp.mean(v, axis=-1, keepdims=True)
        vc = v - mu
        var = jnp.mean(vc * vc, axis=-1, keepdims=True)
        sgv_ref[0, rows, :] = (vc * lax.rsqrt(var + EPS) * lng_ref[...] + lnb_ref[...]).astype(bf16)
        nav_ref[0, :, :, rows] = _values_t(mm(3 * gw))
        naq_ref[0, :, rows] = (mm(gw) * NA_Q_SCALE).T.astype(bf16)
        nak_ref[0, rows, :] = mm(2 * gw).astype(bf16)


def _proj_in(x, mods, layer, ctx, g, w, bd, rope, lng, lnb):
    bsz, s, _ = x.shape
    tm = min(s, PROJ_ROWS)
    row = lambda b, i: (b, i, 0)
    const = lambda b, i: (0, 0)
    of_layer = lambda b, i: (layer, 0, 0)
    pos = lambda b, i: (i, 0)
    out = lambda width: pl.BlockSpec((1, tm, width), row)
    shape = lambda width: jax.ShapeDtypeStruct((bsz, s, width), bf16)
    vt = (pl.BlockSpec((1, N_HEADS, 2 * HEAD_DIM, tm), lambda b, i: (b, 0, 0, i)),
          jax.ShapeDtypeStruct((bsz, N_HEADS, 2 * HEAD_DIM, s), bf16))
    qt = (pl.BlockSpec((1, GROUP_W, tm), lambda b, i: (b, 0, i)), jax.ShapeDtypeStruct((bsz, GROUP_W, s), bf16))
    by_rows = lambda width: (out(width), shape(width))
    outs = (by_rows(2 * GROUP_W), qt, by_rows(GROUP_W), vt, by_rows(GROUP_W), by_rows(GROUP_W), qt,
            by_rows(GROUP_W), vt)
    return pl.pallas_call(
        _proj_in_kernel,
        grid=(bsz, s // tm),
        in_specs=[
            pl.BlockSpec((1, tm, D_MODEL), row),
            _mod_spec(layer, 0, ctx), _mod_spec(layer, 1, ctx),
            pl.BlockSpec((None, 1, D_MODEL), of_layer),
            pl.BlockSpec((None, D_MODEL, D_IN), of_layer),
            pl.BlockSpec((GROUP_W, 2 * GROUP_W), const),
            pl.BlockSpec((tm, GROUP_W), pos), pl.BlockSpec((tm, GROUP_W), pos),
            pl.BlockSpec((tm, GROUP_W), pos), pl.BlockSpec((tm, GROUP_W), pos),
            pl.BlockSpec((None, 1, GROUP_W), of_layer), pl.BlockSpec((None, 1, GROUP_W), of_layer),
        ],
        out_specs=[spec for spec, _ in outs],
        out_shape=[shp for _, shp in outs],
        compiler_params=pltpu.CompilerParams(vmem_limit_bytes=V7X_VMEM_LIMIT),
        name="proj_in_ctx" if ctx else "proj_in",
    )(x, mods, mods, g, w, bd, *rope, lng, lnb)


def _fourier_kernel(c_ref, s_ref, a_ref, o_ref, acc_ref):
    k = pl.program_id(1)
    gw = GROUP_W

    @pl.when(k == 0)
    def _():
        acc_ref[...] = jnp.zeros_like(acc_ref)

    for b in range(a_ref.shape[0]):
        acc_ref[b] += _dot(c_ref[...], a_ref[b, :, :gw]) + _dot(s_ref[...], a_ref[b, :, gw:])

    @pl.when(k == pl.num_programs(1) - 1)
    def _():
        o_ref[...] = acc_ref[...].astype(o_ref.dtype)


def _fourier(cn, sn, acs):
    bsz, s, _ = acs.shape
    tm = min(s, DFT_ROWS)
    tk = min(s, DFT_COLS)
    return pl.pallas_call(
        _fourier_kernel,
        grid=(s // tm, s // tk),
        in_specs=[
            pl.BlockSpec((tm, tk), lambda i, k: (i, k)),
            pl.BlockSpec((tm, tk), lambda i, k: (i, k)),
            pl.BlockSpec((bsz, tk, 2 * GROUP_W), lambda i, k: (0, k, 0)),
        ],
        out_specs=pl.BlockSpec((bsz, tm, GROUP_W), lambda i, k: (0, i, 0)),
        out_shape=jax.ShapeDtypeStruct((bsz, s, GROUP_W), bf16),
        scratch_shapes=[pltpu.VMEM((bsz, tm, GROUP_W), f32)],
        compiler_params=pltpu.CompilerParams(vmem_limit_bytes=V7X_VMEM_LIMIT),
        name="fourier",
    )(cn, sn, acs)


def _na_band_start(i):
    rows = GRID_W
    return jnp.clip(i * NA_Q_ROWS - NA_WIN_ROWS // 2, 0, rows - NA_BAND_ROWS)


def _na_bias_t(rpb, rows):
    depth = rpb.shape[0]
    cidx = np.arange(GRID_W)
    col_start = np.clip(cidx - NA_WIN_COLS // 2, 0, GRID_W - NA_WIN_COLS)
    in_win = (cidx[None, :] >= col_start[:, None]) & (cidx[None, :] < col_start[:, None] + NA_WIN_COLS)
    dc = np.clip(cidx[None, :] - cidx[:, None], 1 - NA_WIN_COLS, NA_WIN_COLS - 1) + NA_WIN_COLS - 1
    pick = (dc.T[None] == np.arange(2 * NA_WIN_COLS - 1)[:, None, None]).astype(np.float32)
    t = jnp.einsum('ldc,ckq->ldkq', rpb.reshape(depth * N_HEADS, *rpb.shape[2:]), pick,
                   precision=lax.Precision.HIGHEST)
    t = jnp.where(in_win.T[None, None], t * LOG2E, MASKED)
    outside = jnp.full_like(t[:, 0], MASKED)
    nblk = rows // NA_Q_ROWS
    cats = []
    for blk in (0, 1, nblk - 1):
        band0 = int(np.clip(blk * NA_Q_ROWS - NA_WIN_ROWS // 2, 0, rows - NA_BAND_ROWS))
        key_rows = []
        for kj in range(NA_BAND_ROWS):
            kr = band0 + kj
            per_q = []
            for qi in range(NA_Q_ROWS):
                r = blk * NA_Q_ROWS + qi
                r0 = int(np.clip(r - NA_WIN_ROWS // 2, 0, rows - NA_WIN_ROWS))
                per_q.append(t[:, kr - r + NA_WIN_ROWS - 1] if r0 <= kr < r0 + NA_WIN_ROWS else outside)
            key_rows.append(jnp.concatenate(per_q, axis=-1))
        cats.append(jnp.concatenate(key_rows, axis=1))
    return jnp.stack(cats).reshape(3 * depth, N_HEADS, NA_BAND_ROWS * GRID_W, NA_Q_ROWS * GRID_W)


def _sgu_kernel(u_ref, v_ref, w_ref, b_ref, o_ref):
    tm = u_ref.shape[1]
    lane = lax.broadcasted_iota(jnp.int32, (SGU_CHUNK, GROUP_W), 1)
    for c in range(tm // SGU_CHUNK):
        rows = slice(c * SGU_CHUNK, (c + 1) * SGU_CHUNK)
        vchunk = v_ref[0, rows, :]
        s = b_ref[...]
        for g in range(N_HEADS):
            in_group = (lane >= g * HEAD_DIM) & (lane < (g + 1) * HEAD_DIM)
            s = jnp.where(in_group, s + _dot(w_ref[g], vchunk), s)
        o_ref[0, rows, :] = (u_ref[0, rows, :].astype(f32) * s).astype(o_ref.dtype)


def _sgu(layer, u, v, w, b_full):
    bsz, s, _ = u.shape
    tm = min(s, ROW_TILE)
    row = pl.BlockSpec((1, tm, GROUP_W), lambda b, i: (b, i, 0))
    return pl.pallas_call(
        _sgu_kernel,
        grid=(bsz, s // tm),
        in_specs=[row, row,
                  pl.BlockSpec((None, N_HEADS, SGU_CHUNK, SGU_CHUNK), lambda b, i: (layer, 0, 0, 0)),
                  pl.BlockSpec((None, SGU_CHUNK, GROUP_W), lambda b, i: (layer, 0, 0))],
        out_specs=row,
        out_shape=jax.ShapeDtypeStruct((bsz, s, GROUP_W), bf16),
        name="sgu",
    )(u, v, w, b_full)


def _attn_t_kernel(*refs, layer, n_maps, post_scale, sources, n_blocks):
    lam_ref, q_ref = refs[:2]
    pos = 2
    i = pl.program_id(1)
    band_new = _na_band_start(jnp.minimum(i, n_blocks - 1)) * GRID_W
    band_old = _na_band_start(jnp.maximum(i - 1, 0)) * GRID_W
    chunks = []
    for n_keys, banded, biased in sources:
        k_ref, vt_ref = refs[pos:pos + 2]
        bias_ref = refs[pos + 2] if biased else None
        pos += 3 if biased else 2
        for off in range(0, n_keys, KEY_CHUNK):
            first = (pl.multiple_of(band_new + off, KEY_CHUNK), pl.multiple_of(band_old + off, KEY_CHUNK))
            chunks.append((k_ref, vt_ref, bias_ref) + (first if banded else (off, off)) + (off,))
    g_ref, o_ref, qm_ref, sa_ref, sb_ref, ma_ref, mb_ref, acc_ref = refs[pos:]

    n_streams = N_HEADS * n_maps
    width = GROUP_W // n_streams
    tq = q_ref.shape[2]
    qt = q_ref[0]
    chan = lax.broadcasted_iota(jnp.int32, qt.shape, 0)
    for j in range(n_streams):
        qm_ref[j] = jnp.where((chan >= j * width) & (chan < (j + 1) * width), qt, jnp.zeros_like(qt))

    @pl.when(i == 0)
    def _():
        sb_ref[...] = jnp.zeros_like(sb_ref)
        mb_ref[...] = jnp.zeros_like(mb_ref)
        acc_ref[...] = jnp.ones_like(acc_ref)

    def step(j_scores, j_values, to_first, values_of_old_block=False):
        s_new, m_new = (sa_ref, ma_ref) if to_first else (sb_ref, mb_ref)
        s_old, m_old = (sb_ref, mb_ref) if to_first else (sa_ref, ma_ref)
        mx = jnp.full((8, tq), MASKED, f32)
        m = m_old[...]
        acc = jnp.zeros((2 * HEAD_DIM, tq), f32)
        for c, (k_ref, vt_ref, bias_ref, first_new, first_old, bias_first) in enumerate(chunks):
            rows = slice(c * KEY_CHUNK, (c + 1) * KEY_CHUNK)
            if j_scores is not None:
                s = _dot(k_ref[0, pl.ds(first_new, KEY_CHUNK), :], qm_ref[j_scores])
                if bias_ref is not None:
                    s = s + bias_ref[j_scores // n_maps, bias_first:bias_first + KEY_CHUNK, :]
                s_new[rows, :] = s
                mx = jnp.maximum(mx, jnp.max(s.reshape(KEY_CHUNK // 8, 8, tq), axis=0))
            p = jnp.exp2(s_old[rows, :] - m).astype(bf16)
            first = first_old if values_of_old_block else first_new
            acc = acc + _dot(vt_ref[0, j_values // n_maps, :, pl.ds(first, KEY_CHUNK)], p)
        if j_scores is not None:
            m_new[...] = jnp.max(mx, axis=0, keepdims=True)
        acc_ref[j_values] = acc

    @pl.when(i < n_blocks)
    def _():
        step(0, n_streams - 1, True, values_of_old_block=True)
        _attn_finish(lam_ref, g_ref, acc_ref, o_ref, layer, n_maps, post_scale)
        for j in range(1, n_streams):
            step(j, j - 1, j % 2 == 0)

    @pl.when(i == n_blocks)
    def _():
        step(None, n_streams - 1, True, values_of_old_block=True)
        _attn_finish(lam_ref, g_ref, acc_ref, o_ref, layer, n_maps, post_scale)


def _attn_finish(lam_ref, g_ref, acc_ref, o_ref, layer, n_maps, post_scale):
    outs = []
    for h in range(N_HEADS):
        if n_maps == 1:
            a = acc_ref[h]
            o = a[:HEAD_DIM] / a[HEAD_DIM:HEAD_DIM + 1]
        else:
            a1 = acc_ref[2 * h]
            a2 = acc_ref[2 * h + 1]
            o = a1[:HEAD_DIM] / a1[HEAD_DIM:HEAD_DIM + 1] - lam_ref[layer] * (a2[:HEAD_DIM] / a2[HEAD_DIM:HEAD_DIM + 1])
            r = lax.rsqrt(jnp.mean(o * o, axis=0, keepdims=True) + EPS)
            o = (o * r) * g_ref[...] * post_scale
        outs.append(o)
    o_ref[0] = jnp.concatenate(outs, axis=0).T.astype(o_ref.dtype)


def _attn_t(name, layer, q, srcs, lam, g_rows, n_maps, post_scale):
    bsz, _, s = q.shape
    tq = min(s, ATTN_Q_BLOCK)
    nblk = s // tq
    n_streams = N_HEADS * n_maps
    n_total = sum(src[3] for src in srcs)
    in_specs = [
        pl.BlockSpec(memory_space=pltpu.SMEM),
        pl.BlockSpec((1, GROUP_W, tq), lambda b, i: (b, 0, jnp.minimum(i, nblk - 1))),
    ]
    args = [lam, q]
    for k, vt, bias, n_used, banded in srcs:
        in_specs += [
            pl.BlockSpec((1, k.shape[1], GROUP_W), lambda b, i: (b, 0, 0)),
            pl.BlockSpec((1, N_HEADS, 2 * HEAD_DIM, k.shape[1]), lambda b, i: (b, 0, 0, 0)),
        ]
        args += [k, vt]
        if bias is not None:
            n_layers = bias.shape[0] // 3
            in_specs.append(pl.BlockSpec(
                (None, N_HEADS, n_used, tq),
                lambda b, i: (jnp.where(i == 0, 0, jnp.where(i >= nblk - 1, 2, 1)) * n_layers + layer, 0, 0, 0)))
            args.append(bias)
    in_specs.append(pl.BlockSpec((None, HEAD_DIM, tq), lambda b, i: (layer, 0, 0)))
    args.append(g_rows)
    return pl.pallas_call(
        partial(_attn_t_kernel, layer=layer, n_maps=n_maps, post_scale=post_scale,
                sources=tuple((src[3], src[4], src[2] is not None) for src in srcs), n_blocks=nblk),
        grid=(bsz, nblk + 1),
        in_specs=in_specs,
        out_specs=pl.BlockSpec((1, tq, GROUP_W), lambda b, i: (b, jnp.maximum(i - 1, 0), 0)),
        out_shape=jax.ShapeDtypeStruct((bsz, s, GROUP_W), bf16),
        scratch_shapes=[
            pltpu.VMEM((n_streams, GROUP_W, tq), bf16),
            pltpu.VMEM((n_total, tq), f32), pltpu.VMEM((n_total, tq), f32),
            pltpu.VMEM((1, tq), f32), pltpu.VMEM((1, tq), f32),
            pltpu.VMEM((n_streams, 2 * HEAD_DIM, tq), f32),
        ],
        compiler_params=pltpu.CompilerParams(vmem_limit_bytes=V7X_VMEM_LIMIT),
        name=name,
    )(*args)


def _post_kernel(x_ref, ya_ref, yb_ref, yc_ref, yd_ref, g1_ref, sh_ref, sc_ref, g2_ref, ng_ref, wo_ref, w1_ref,
                 w2_ref, fg_ref, o_ref, *, final):
    y = jnp.concatenate([ya_ref[0], yb_ref[0], yc_ref[0], yd_ref[0]], axis=-1)
    x = x_ref[0] + g1_ref[0] * _dot(y, wo_ref[...])
    r = lax.rsqrt(jnp.mean(x * x, axis=-1, keepdims=True) + EPS)
    h = (x * r) * ng_ref[...]
    hb = (h * (1.0 + sc_ref[0]) + sh_ref[0]).astype(bf16)
    acc = jnp.zeros(x.shape, f32)
    step = D_MODEL
    for j in range(D_FF // step):
        t = jnp.maximum(_dot(hb, w1_ref[:, j * step:(j + 1) * step]), 0.0)
        acc = acc + _dot((t * t).astype(bf16), w2_ref[j * step:(j + 1) * step, :])
    x = x + g2_ref[0] * acc
    if final:
        r = lax.rsqrt(jnp.mean(x * x, axis=-1, keepdims=True) + EPS)
        x = (x * r) * fg_ref[...]
    o_ref[0] = x


def _post(x, ys, mods, layer, ctx, ng, wo, w1, w2, fg, final):
    bsz, s, _ = x.shape
    tm = min(s, ROW_TILE)
    row = lambda width: pl.BlockSpec((1, tm, width), lambda b, i: (b, i, 0))
    of_layer = lambda shape: pl.BlockSpec((None,) + shape, lambda b, i: (layer, 0, 0), pipeline_mode=pl.Buffered(1))
    return pl.pallas_call(
        partial(_post_kernel, final=final),
        grid=(bsz, s // tm),
        in_specs=[
            row(D_MODEL), row(GROUP_W), row(GROUP_W), row(GROUP_W), row(GROUP_W),
            _mod_spec(layer, 2, ctx), _mod_spec(layer, 3, ctx), _mod_spec(layer, 4, ctx), _mod_spec(layer, 5, ctx),
            of_layer((1, D_MODEL)), of_layer((D_MODEL, D_MODEL)), of_layer((D_MODEL, D_FF)),
            of_layer((D_FF, D_MODEL)),
            pl.BlockSpec((1, D_MODEL), lambda b, i: (0, 0)),
        ],
        out_specs=row(D_MODEL),
        out_shape=jax.ShapeDtypeStruct((bsz, s, D_MODEL), f32),
        compiler_params=pltpu.CompilerParams(vmem_limit_bytes=V7X_VMEM_LIMIT),
        name="post_ctx" if ctx else "post",
    )(x, *ys, mods, mods, mods, mods, ng, wo, w1, w2, fg)


def _channel_dft(n_pos):
    c = np.arange(HEAD_DIM)
    ang = 2.0 * np.pi * ((c[:, None] * c[None, :]) % HEAD_DIM) / HEAD_DIM
    scale = 1.0 / math.sqrt(n_pos * HEAD_DIM)
    bd = np.zeros((GROUP_W, 2 * GROUP_W), np.float32)
    for g in range(N_HEADS):
        sl = slice(g * HEAD_DIM, (g + 1) * HEAD_DIM)
        bd[sl, sl] = np.cos(ang) * scale
        bd[sl, GROUP_W + g * HEAD_DIM:GROUP_W + (g + 1) * HEAD_DIM] = -np.sin(ang) * scale
    return jnp.asarray(bd, bf16)


def _position_dft(n_pos):
    hi = max(n_pos // GRID_W, 1)
    lo = n_pos // hi
    col = jnp.arange(n_pos, dtype=jnp.int32)[None, :]
    a = 2.0 * np.pi * ((col * jnp.arange(hi, dtype=jnp.int32)[:, None] * lo) % n_pos).astype(f32) / n_pos
    b = 2.0 * np.pi * ((col * jnp.arange(lo, dtype=jnp.int32)[:, None]) % n_pos).astype(f32) / n_pos
    ca, sa, cb, sb = jnp.cos(a)[:, None, :], jnp.sin(a)[:, None, :], jnp.cos(b)[None, :, :], jnp.sin(b)[None, :, :]
    cn = (ca * cb - sa * sb).reshape(n_pos, n_pos)
    sn = (sa * cb + ca * sb).reshape(n_pos, n_pos)
    return cn.astype(bf16), sn.astype(bf16)


def _rope_tables(n):
    t = jnp.arange(n)
    rows = (t // GRID_W).astype(f32)
    cols = (t % GRID_W).astype(f32)
    n_freq = DIFF_QK_DIM // 4
    inv = ROPE_BASE ** (-jnp.arange(n_freq, dtype=f32) / n_freq)
    ang = jnp.concatenate([rows[:, None] * inv, cols[:, None] * inv], axis=-1)
    lane = np.arange(GROUP_W) % DIFF_QK_DIM
    src = (lane // 16) * n_freq + lane % n_freq
    sign = np.where((lane % 16) < n_freq, -1.0, 1.0).astype(np.float32)
    return jnp.cos(ang)[:, src], jnp.sin(ang)[:, src] * sign


def kernel(x, c, ctx, c_ctx, ada_w, ada_b, norm1_g, norm2_g, w_in, w_out, na_rpb, sgu_ln_g, sgu_ln_b, sgu_w, sgu_b,
           diff_lq1, diff_lk1, diff_lq2, diff_lk2, diff_subln_g, w_ff1, w_ff2, final_g):
    bsz, n, _ = x.shape
    n_ctx = ctx.shape[1]
    depth = w_in.shape[0]
    rows = n // GRID_W
    assert bsz <= COND_ROWS // 2

    cond = jnp.zeros((COND_ROWS, D_MODEL), f32).at[:bsz].set(c).at[COND_ROWS // 2].set(c_ctx)
    mods = _ada_mods(cond, ada_w, ada_b).reshape(depth * COND_ROWS * 6, 1, D_MODEL)

    cos_l, sin_l = _rope_tables(n)
    rope_lat = (cos_l * DIFF_Q_SCALE, sin_l * DIFF_Q_SCALE, cos_l, sin_l)
    ones = jnp.ones((n_ctx, GROUP_W), f32)
    rope_ctx = (ones * DIFF_Q_SCALE, ones * 0.0, ones, ones * 0.0)
    bd_lat, bd_ctx = _channel_dft(n), _channel_dft(n_ctx)
    dft_lat, dft_ctx = _position_dft(n), _position_dft(n_ctx)

    w_in_b, w_out_b = w_in.astype(bf16), w_out.astype(bf16)
    w1_b, w2_b, sgu_w_b = w_ff1.astype(bf16), w_ff2.astype(bf16), sgu_w.astype(bf16)
    fg = final_g.reshape(1, D_MODEL)
    g1 = norm1_g.reshape(depth, 1, D_MODEL)
    g2 = norm2_g.reshape(depth, 1, D_MODEL)
    lng = sgu_ln_g.reshape(depth, 1, GROUP_W)
    lnb = sgu_ln_b.reshape(depth, 1, GROUP_W)
    sgu_bias = jnp.repeat(sgu_b.transpose(0, 2, 1), HEAD_DIM, axis=2)
    subln = jnp.broadcast_to(diff_subln_g[:, :, None], (depth, HEAD_DIM, min(n, ATTN_Q_BLOCK)))
    lam_init = [0.8 - 0.6 * math.exp(-0.3 * l) for l in range(depth)]
    lam = (jnp.exp(jnp.sum(diff_lq1.astype(f32) * diff_lk1.astype(f32), axis=-1))
           - jnp.exp(jnp.sum(diff_lq2.astype(f32) * diff_lk2.astype(f32), axis=-1)) + jnp.asarray(lam_init, f32))
    na_bias = _na_bias_t(na_rpb, rows)
    band = NA_BAND_ROWS * GRID_W
    cx = ctx

    for l in range(depth):
        last = l == depth - 1
        post_scale = 1.0 - lam_init[l]

        p_lat = _proj_in(x, mods, l, False, g1, w_in_b, bd_lat, rope_lat, lng, lnb)
        p_ctx = _proj_in(cx, mods, l, True, g1, w_in_b, bd_ctx, rope_ctx, lng, lnb)
        acs, naq, nak, nav, sgu_u, sgu_v, dfq, dfk, dfv = p_lat
        c_acs, c_naq, c_nak, c_nav, c_sgu_u, c_sgu_v, c_dfq, c_dfk, c_dfv = p_ctx

        y_a = _fourier(*dft_lat, acs)
        ctx_na = (c_nak, c_nav, None, n_ctx, False)
        ctx_df = (c_dfk, c_dfv, None, n_ctx, False)
        y_b = _attn_t("na", l, naq, [(nak, nav, na_bias, band, True), ctx_na], lam, subln, 1, 1.0)
        y_c = _sgu(l, sgu_u, sgu_v, sgu_w_b, sgu_bias)
        y_d = _attn_t("diff", l, dfq, [(dfk, dfv, None, n, False), ctx_df], lam, subln, 2, post_scale)

        if not last:
            yc_a = _fourier(*dft_ctx, c_acs)
            yc_b = _attn_t("dense_ctx", l, c_naq, [ctx_na], lam, subln, 1, 1.0)
            yc_c = _sgu(l, c_sgu_u, c_sgu_v, sgu_w_b, sgu_bias)
            yc_d = _attn_t("diff_ctx", l, c_dfq, [ctx_df], lam, subln, 2, post_scale)
            cx = _post(cx, (yc_a, yc_b, yc_c, yc_d), mods, l, True, g2, w_out_b, w1_b, w2_b, fg, False)

        x = _post(x, (y_a, y_b, y_c, y_d), mods, l, False, g2, w_out_b, w1_b, w2_b, fg, last)
    return x
```

```python
import math
from functools import partial

import numpy as np
import jax
import jax.numpy as jnp
from jax import lax
from jax.experimental import pallas as pl
from jax.experimental.pallas import tpu as pltpu

D_MODEL = 1024
GRID_W = 64
N_HEADS = 4
GROUP_W = 256
HEAD_DIM = 64
NA_WIN_ROWS = 8
NA_WIN_COLS = 16
SGU_CHUNK = 128
DIFF_QK_DIM = 32
ROPE_BASE = 10000.0
D_FF = 4 * D_MODEL
D_IN = 9 * GROUP_W
EPS = 1e-6

LOG2E = 1.4426950408889634
NA_Q_SCALE = HEAD_DIM ** -0.5 * LOG2E
DIFF_Q_SCALE = DIFF_QK_DIM ** -0.5 * LOG2E
MASKED = -1e30
COND_ROWS = 16
KEY_CHUNK = 256
NA_Q_ROWS = 8
NA_BAND_ROWS = 16
ATTN_Q_BLOCK = NA_Q_ROWS * GRID_W
PROJ_ROWS = 1024
PROJ_SUB_ROWS = 512
ROW_TILE = 1024
POST_SUB_ROWS = 512
SGU_ROWS = 4096
ADA_COLS = 1536
DFT_ROWS, DFT_COLS = 1024, 1024
V7X_VMEM_LIMIT = 56 * 1024 * 1024

f32 = jnp.float32
bf16 = jnp.bfloat16


def _dot(a, b):
    return jnp.dot(a, b, preferred_element_type=f32)


def _ada_kernel(cond_ref, w_ref, b_ref, o_ref):
    c = cond_ref[...]
    s = (c * jax.nn.sigmoid(c)).astype(bf16)
    o_ref[0] = _dot(s, w_ref[0].astype(bf16)) + b_ref[0]


def _ada_mods(cond, ada_w, ada_b):
    depth = ada_w.shape[0]
    tn = ADA_COLS
    return pl.pallas_call(
        _ada_kernel,
        grid=(depth, 6 * D_MODEL // tn),
        in_specs=[
            pl.BlockSpec((COND_ROWS, D_MODEL), lambda l, j: (0, 0)),
            pl.BlockSpec((1, D_MODEL, tn), lambda l, j: (l, 0, j)),
            pl.BlockSpec((1, 1, tn), lambda l, j: (l, 0, j)),
        ],
        out_specs=pl.BlockSpec((1, COND_ROWS, tn), lambda l, j: (l, 0, j)),
        out_shape=jax.ShapeDtypeStruct((depth, COND_ROWS, 6 * D_MODEL), f32),
        compiler_params=pltpu.CompilerParams(vmem_limit_bytes=V7X_VMEM_LIMIT),
        name="ada",
    )(cond, ada_w, ada_b.reshape(depth, 1, 6 * D_MODEL))


def _mod_spec(layer, which, ctx):
    if ctx:
        return pl.BlockSpec((1, 1, D_MODEL), lambda b, i: ((layer * COND_ROWS + COND_ROWS // 2) * 6 + which, 0, 0))
    return pl.BlockSpec((1, 1, D_MODEL), lambda b, i: ((layer * COND_ROWS + b) * 6 + which, 0, 0))


def _rope_partner(x):
    lane = lax.broadcasted_iota(jnp.int32, x.shape, 1)
    first_half = (lane % 16) < 8
    n = x.shape[1]
    return jnp.where(first_half, pltpu.roll(x, n - 8, 1), pltpu.roll(x, 8, 1))


def _values_t(v):
    vt = v.T.astype(bf16).reshape(N_HEADS, HEAD_DIM, v.shape[0])
    return jnp.concatenate([vt, jnp.ones_like(vt)], axis=1)


def _proj_in_kernel(x_ref, sh_ref, sc_ref, g_ref, w_ref, bd_ref, cq_ref, sq_ref, ck_ref, sk_ref, lng_ref, lnb_ref,
                    acs_ref, naq_ref, nak_ref, nav_ref, sgu_ref, sgv_ref, dfq_ref, dfk_ref, dfv_ref):
    gw = GROUP_W
    gain = g_ref[...] * (1.0 + sc_ref[0])
    sub = min(x_ref.shape[1], PROJ_SUB_ROWS)
    for r0 in range(0, x_ref.shape[1], sub):
        rows = slice(r0, r0 + sub)
        x = x_ref[0, rows, :]
        r = lax.rsqrt(jnp.mean(x * x, axis=-1, keepdims=True) + EPS)
        hb = ((x * r) * gain + sh_ref[0]).astype(bf16)

        def mm(col, width=gw):
            return _dot(hb, w_ref[:, col:col + width])

        acs_ref[0, rows, :] = _dot(mm(0).astype(bf16), bd_ref[...]).astype(bf16)
        q = mm(6 * gw)
        dfq_ref[0, :, rows] = (q * cq_ref[rows, :] + _rope_partner(q) * sq_ref[rows, :]).T.astype(bf16)
        k = mm(7 * gw)
        dfk_ref[0, rows, :] = (k * ck_ref[rows, :] + _rope_partner(k) * sk_ref[rows, :]).astype(bf16)
        dfv_ref[0, :, :, rows] = _values_t(mm(8 * gw))
        uv = jax.nn.gelu(mm(4 * gw, 2 * gw), approximate=True)
        sgu_ref[0, rows, :] = uv[:, :gw].astype(bf16)
        v = uv[:, gw:]
        mu = jnp.mean(v, axis=-1, keepdims=True)
        vc = v - mu
        var = jnp.mean(vc * vc, axis=-1, keepdims=True)
        sgv_ref[0, rows, :] = (vc * lax.rsqrt(var + EPS) * lng_ref[...] + lnb_ref[...]).astype(bf16)
        nav_ref[0, :, :, rows] = _values_t(mm(3 * gw))
        naq_ref[0, :, rows] = (mm(gw) * NA_Q_SCALE).T.astype(bf16)
        nak_ref[0, rows, :] = mm(2 * gw).astype(bf16)


def _proj_in(x, mods, layer, ctx, g, w, bd, rope, lng, lnb):
    bsz, s, _ = x.shape
    tm = min(s, PROJ_ROWS)
    row = lambda b, i: (b, i, 0)
    const = lambda b, i: (0, 0)
    of_layer = lambda b, i: (layer, 0, 0)
    pos = lambda b, i: (i, 0)
    out = lambda width: pl.BlockSpec((1, tm, width), row)
    shape = lambda width: jax.ShapeDtypeStruct((bsz, s, width), bf16)
    vt = (pl.BlockSpec((1, N_HEADS, 2 * HEAD_DIM, tm), lambda b, i: (b, 0, 0, i)),
          jax.ShapeDtypeStruct((bsz, N_HEADS, 2 * HEAD_DIM, s), bf16))
    qt = (pl.BlockSpec((1, GROUP_W, tm), lambda b, i: (b, 0, i)), jax.ShapeDtypeStruct((bsz, GROUP_W, s), bf16))
    by_rows = lambda width: (out(width), shape(width))
    outs = (by_rows(2 * GROUP_W), qt, by_rows(GROUP_W), vt, by_rows(GROUP_W), by_rows(GROUP_W), qt,
            by_rows(GROUP_W), vt)
    return pl.pallas_call(
        _proj_in_kernel,
        grid=(bsz, s // tm),
        in_specs=[
            pl.BlockSpec((1, tm, D_MODEL), row),
            _mod_spec(layer, 0, ctx), _mod_spec(layer, 1, ctx),
            pl.BlockSpec((None, 1, D_MODEL), of_layer),
            pl.BlockSpec((None, D_MODEL, D_IN), of_layer),
            pl.BlockSpec((GROUP_W, 2 * GROUP_W), const),
            pl.BlockSpec((tm, GROUP_W), pos), pl.BlockSpec((tm, GROUP_W), pos),
            pl.BlockSpec((tm, GROUP_W), pos), pl.BlockSpec((tm, GROUP_W), pos),
            pl.BlockSpec((None, 1, GROUP_W), of_layer), pl.BlockSpec((None, 1, GROUP_W), of_layer),
        ],
        out_specs=[spec for spec, _ in outs],
        out_shape=[shp for _, shp in outs],
        compiler_params=pltpu.CompilerParams(vmem_limit_bytes=V7X_VMEM_LIMIT),
        name="proj_in_ctx" if ctx else "proj_in",
    )(x, mods, mods, g, w, bd, *rope, lng, lnb)


def _fourier_kernel(c_ref, s_ref, a_ref, o_ref, acc_ref):
    k = pl.program_id(1)
    gw = GROUP_W

    @pl.when(k == 0)
    def _():
        acc_ref[...] = jnp.zeros_like(acc_ref)

    for b in range(a_ref.shape[0]):
        acc_ref[b] += _dot(c_ref[...], a_ref[b, :, :gw]) + _dot(s_ref[...], a_ref[b, :, gw:])

    @pl.when(k == pl.num_programs(1) - 1)
    def _():
        o_ref[...] = acc_ref[...].astype(o_ref.dtype)


def _fourier(cn, sn, acs):
    bsz, s, _ = acs.shape
    tm = min(s, DFT_ROWS)
    tk = min(s, DFT_COLS)
    return pl.pallas_call(
        _fourier_kernel,
        grid=(s // tm, s // tk),
        in_specs=[
            pl.BlockSpec((tm, tk), lambda i, k: (i, k)),
            pl.BlockSpec((tm, tk), lambda i, k: (i, k)),
            pl.BlockSpec((bsz, tk, 2 * GROUP_W), lambda i, k: (0, k, 0)),
        ],
        out_specs=pl.BlockSpec((bsz, tm, GROUP_W), lambda i, k: (0, i, 0)),
        out_shape=jax.ShapeDtypeStruct((bsz, s, GROUP_W), bf16),
        scratch_shapes=[pltpu.VMEM((bsz, tm, GROUP_W), f32)],
        compiler_params=pltpu.CompilerParams(vmem_limit_bytes=V7X_VMEM_LIMIT),
        name="fourier",
    )(cn, sn, acs)


def _na_band_start(i):
    rows = GRID_W
    return jnp.clip(i * NA_Q_ROWS - NA_WIN_ROWS // 2, 0, rows - NA_BAND_ROWS)


def _na_bias_t(rpb, rows):
    depth = rpb.shape[0]
    cidx = np.arange(GRID_W)
    col_start = np.clip(cidx - NA_WIN_COLS // 2, 0, GRID_W - NA_WIN_COLS)
    in_win = (cidx[None, :] >= col_start[:, None]) & (cidx[None, :] < col_start[:, None] + NA_WIN_COLS)
    dc = np.clip(cidx[None, :] - cidx[:, None], 1 - NA_WIN_COLS, NA_WIN_COLS - 1) + NA_WIN_COLS - 1
    pick = (dc.T[None] == np.arange(2 * NA_WIN_COLS - 1)[:, None, None]).astype(np.float32)
    t = jnp.einsum('ldc,ckq->ldkq', rpb.reshape(depth * N_HEADS, *rpb.shape[2:]), pick,
                   precision=lax.Precision.HIGHEST)
    t = jnp.where(in_win.T[None, None], t * LOG2E, MASKED)
    outside = jnp.full_like(t[:, 0], MASKED)
    nblk = rows // NA_Q_ROWS
    cats = []
    for blk in (0, 1, nblk - 1):
        band0 = int(np.clip(blk * NA_Q_ROWS - NA_WIN_ROWS // 2, 0, rows - NA_BAND_ROWS))
        key_rows = []
        for kj in range(NA_BAND_ROWS):
            kr = band0 + kj
            per_q = []
            for qi in range(NA_Q_ROWS):
                r = blk * NA_Q_ROWS + qi
                r0 = int(np.clip(r - NA_WIN_ROWS // 2, 0, rows - NA_WIN_ROWS))
                per_q.append(t[:, kr - r + NA_WIN_ROWS - 1] if r0 <= kr < r0 + NA_WIN_ROWS else outside)
            key_rows.append(jnp.concatenate(per_q, axis=-1))
        cats.append(jnp.concatenate(key_rows, axis=1))
    return jnp.stack(cats).reshape(3 * depth, N_HEADS, NA_BAND_ROWS * GRID_W, NA_Q_ROWS * GRID_W)


def _sgu_kernel(u_ref, v_ref, w_ref, b_ref, o_ref):
    tm = u_ref.shape[1]
    lane = lax.broadcasted_iota(jnp.int32, (SGU_CHUNK, GROUP_W), 1)
    for c in range(tm // SGU_CHUNK):
        rows = slice(c * SGU_CHUNK, (c + 1) * SGU_CHUNK)
        vchunk = v_ref[0, rows, :]
        s = b_ref[...]
        for g in range(N_HEADS):
            in_group = (lane >= g * HEAD_DIM) & (lane < (g + 1) * HEAD_DIM)
            s = jnp.where(in_group, s + _dot(w_ref[g], vchunk), s)
        o_ref[0, rows, :] = (u_ref[0, rows, :].astype(f32) * s).astype(o_ref.dtype)


def _sgu(layer, u, v, w, b_full):
    bsz, s, _ = u.shape
    tm = min(s, SGU_ROWS)
    row = pl.BlockSpec((1, tm, GROUP_W), lambda b, i: (b, i, 0))
    return pl.pallas_call(
        _sgu_kernel,
        grid=(bsz, s // tm),
        in_specs=[row, row,
                  pl.BlockSpec((None, N_HEADS, SGU_CHUNK, SGU_CHUNK), lambda b, i: (layer, 0, 0, 0)),
                  pl.BlockSpec((None, SGU_CHUNK, GROUP_W), lambda b, i: (layer, 0, 0))],
        out_specs=row,
        out_shape=jax.ShapeDtypeStruct((bsz, s, GROUP_W), bf16),
        name="sgu",
    )(u, v, w, b_full)


def _attn_t_kernel(*refs, layer, n_maps, post_scale, sources, n_blocks):
    lam_ref, q_ref = refs[:2]
    pos = 2
    i = pl.program_id(1)
    band_new = _na_band_start(jnp.minimum(i, n_blocks - 1)) * GRID_W
    band_old = _na_band_start(jnp.maximum(i - 1, 0)) * GRID_W
    chunks = []
    for n_keys, banded, biased in sources:
        k_ref, vt_ref = refs[pos:pos + 2]
        bias_ref = refs[pos + 2] if biased else None
        pos += 3 if biased else 2
        for off in range(0, n_keys, KEY_CHUNK):
            first = (pl.multiple_of(band_new + off, KEY_CHUNK), pl.multiple_of(band_old + off, KEY_CHUNK))
            chunks.append((k_ref, vt_ref, bias_ref) + (first if banded else (off, off)) + (off,))
    g_ref, o_ref, qm_ref, sa_ref, sb_ref, ma_ref, mb_ref, acc_ref = refs[pos:]

    n_streams = N_HEADS * n_maps
    width = GROUP_W // n_streams
    tq = q_ref.shape[2]
    qt = q_ref[0]
    chan = lax.broadcasted_iota(jnp.int32, qt.shape, 0)
    for j in range(n_streams):
        qm_ref[j] = jnp.where((chan >= j * width) & (chan < (j + 1) * width), qt, jnp.zeros_like(qt))

    def step(j_scores, j_values, to_first, values_of_old_block=False):
        s_new, m_new = (sa_ref, ma_ref) if to_first else (sb_ref, mb_ref)
        s_old, m_old = (sb_ref, mb_ref) if to_first else (sa_ref, ma_ref)
        mx = jnp.full((8, tq), MASKED, f32)
        if j_values is not None:
            m = m_old[...]
            acc = jnp.zeros((2 * HEAD_DIM, tq), f32)
        for c, (k_ref, vt_ref, bias_ref, first_new, first_old, bias_first) in enumerate(chunks):
            rows = slice(c * KEY_CHUNK, (c + 1) * KEY_CHUNK)
            s = _dot(k_ref[0, pl.ds(first_new, KEY_CHUNK), :], qm_ref[j_scores])
            if bias_ref is not None:
                s = s + bias_ref[j_scores // n_maps, bias_first:bias_first + KEY_CHUNK, :]
            s_new[rows, :] = s
            mx = jnp.maximum(mx, jnp.max(s.reshape(KEY_CHUNK // 8, 8, tq), axis=0))
            if j_values is not None:
                p = jnp.exp2(s_old[rows, :] - m).astype(bf16)
                first = first_old if values_of_old_block else first_new
                acc = acc + _dot(vt_ref[0, j_values // n_maps, :, pl.ds(first, KEY_CHUNK)], p)
        m_new[...] = jnp.max(mx, axis=0, keepdims=True)
        if j_values is not None:
            acc_ref[j_values] = acc

    @pl.when(i == 0)
    def _():
        step(0, None, True)

    @pl.when(i > 0)
    def _():
        step(0, n_streams - 1, True, values_of_old_block=True)
        _attn_finish(lam_ref, g_ref, acc_ref, o_ref, layer, n_maps, post_scale)

    @pl.when(i < n_blocks)
    def _():
        for j in range(1, n_streams):
            step(j, j - 1, j % 2 == 0)


def _attn_finish(lam_ref, g_ref, acc_ref, o_ref, layer, n_maps, post_scale):
    outs = []
    for h in range(N_HEADS):
        if n_maps == 1:
            a = acc_ref[h]
            o = a[:HEAD_DIM] / a[HEAD_DIM:HEAD_DIM + 1]
        else:
            a1 = acc_ref[2 * h]
            a2 = acc_ref[2 * h + 1]
            o = a1[:HEAD_DIM] / a1[HEAD_DIM:HEAD_DIM + 1] - lam_ref[layer] * (a2[:HEAD_DIM] / a2[HEAD_DIM:HEAD_DIM + 1])
            r = lax.rsqrt(jnp.mean(o * o, axis=0, keepdims=True) + EPS)
            o = (o * r) * g_ref[...] * post_scale
        outs.append(o)
    o_ref[0] = jnp.concatenate(outs, axis=0).T.astype(o_ref.dtype)


def _attn_t(name, layer, q, srcs, lam, g_rows, n_maps, post_scale):
    bsz, _, s = q.shape
    tq = min(s, ATTN_Q_BLOCK)
    nblk = s // tq
    n_streams = N_HEADS * n_maps
    n_total = sum(src[3] for src in srcs)
    in_specs = [
        pl.BlockSpec(memory_space=pltpu.SMEM),
        pl.BlockSpec((1, GROUP_W, tq), lambda b, i: (b, 0, jnp.minimum(i, nblk - 1))),
    ]
    args = [lam, q]
    for k, vt, bias, n_used, banded in srcs:
        in_specs += [
            pl.BlockSpec((1, k.shape[1], GROUP_W), lambda b, i: (b, 0, 0)),
            pl.BlockSpec((1, N_HEADS, 2 * HEAD_DIM, k.shape[1]), lambda b, i: (b, 0, 0, 0)),
        ]
        args += [k, vt]
        if bias is not None:
            n_layers = bias.shape[0] // 3
            in_specs.append(pl.BlockSpec(
                (None, N_HEADS, n_used, tq),
                lambda b, i: (jnp.where(i == 0, 0, jnp.where(i >= nblk - 1, 2, 1)) * n_layers + layer, 0, 0, 0)))
            args.append(bias)
    in_specs.append(pl.BlockSpec((None, HEAD_DIM, tq), lambda b, i: (layer, 0, 0)))
    args.append(g_rows)
    return pl.pallas_call(
        partial(_attn_t_kernel, layer=layer, n_maps=n_maps, post_scale=post_scale,
                sources=tuple((src[3], src[4], src[2] is not None) for src in srcs), n_blocks=nblk),
        grid=(bsz, nblk + 1),
        in_specs=in_specs,
        out_specs=pl.BlockSpec((1, tq, GROUP_W), lambda b, i: (b, jnp.maximum(i - 1, 0), 0)),
        out_shape=jax.ShapeDtypeStruct((bsz, s, GROUP_W), bf16),
        scratch_shapes=[
            pltpu.VMEM((n_streams, GROUP_W, tq), bf16),
            pltpu.VMEM((n_total, tq), f32), pltpu.VMEM((n_total, tq), f32),
            pltpu.VMEM((1, tq), f32), pltpu.VMEM((1, tq), f32),
            pltpu.VMEM((n_streams, 2 * HEAD_DIM, tq), f32),
        ],
        compiler_params=pltpu.CompilerParams(vmem_limit_bytes=V7X_VMEM_LIMIT),
        name=name,
    )(*args)


def _post_kernel(x_ref, ya_ref, yb_ref, yc_ref, yd_ref, g1_ref, sh_ref, sc_ref, g2_ref, ng_ref, wo_ref, w1_ref,
                 w2_ref, fg_ref, o_ref, *, final):
    sub = min(x_ref.shape[1], POST_SUB_ROWS)
    for r0 in range(0, x_ref.shape[1], sub):
        rows = slice(r0, r0 + sub)
        y = jnp.concatenate([ya_ref[0, rows, :], yb_ref[0, rows, :], yc_ref[0, rows, :], yd_ref[0, rows, :]], axis=-1)
        x = x_ref[0, rows, :] + g1_ref[0] * _dot(y, wo_ref[...])
        r = lax.rsqrt(jnp.mean(x * x, axis=-1, keepdims=True) + EPS)
        h = (x * r) * ng_ref[...]
        hb = (h * (1.0 + sc_ref[0]) + sh_ref[0]).astype(bf16)
        acc = jnp.zeros(x.shape, f32)
        step = D_MODEL
        for j in range(D_FF // step):
            t = jnp.maximum(_dot(hb, w1_ref[:, j * step:(j + 1) * step]), 0.0)
            acc = acc + _dot((t * t).astype(bf16), w2_ref[j * step:(j + 1) * step, :])
        x = x + g2_ref[0] * acc
        if final:
            r = lax.rsqrt(jnp.mean(x * x, axis=-1, keepdims=True) + EPS)
            x = (x * r) * fg_ref[...]
        o_ref[0, rows, :] = x


def _post(x, ys, mods, layer, ctx, ng, wo, w1, w2, fg, final):
    bsz, s, _ = x.shape
    tm = min(s, ROW_TILE)
    row = lambda width: pl.BlockSpec((1, tm, width), lambda b, i: (b, i, 0))
    of_layer = lambda shape: pl.BlockSpec((None,) + shape, lambda b, i: (layer, 0, 0), pipeline_mode=pl.Buffered(1))
    return pl.pallas_call(
        partial(_post_kernel, final=final),
        grid=(bsz, s // tm),
        in_specs=[
            row(D_MODEL), row(GROUP_W), row(GROUP_W), row(GROUP_W), row(GROUP_W),
            _mod_spec(layer, 2, ctx), _mod_spec(layer, 3, ctx), _mod_spec(layer, 4, ctx), _mod_spec(layer, 5, ctx),
            of_layer((1, D_MODEL)), of_layer((D_MODEL, D_MODEL)), of_layer((D_MODEL, D_FF)),
            of_layer((D_FF, D_MODEL)),
            pl.BlockSpec((1, D_MODEL), lambda b, i: (0, 0)),
        ],
        out_specs=row(D_MODEL),
        out_shape=jax.ShapeDtypeStruct((bsz, s, D_MODEL), f32),
        compiler_params=pltpu.CompilerParams(vmem_limit_bytes=V7X_VMEM_LIMIT),
        name="post_ctx" if ctx else "post",
    )(x, *ys, mods, mods, mods, mods, ng, wo, w1, w2, fg)


def _channel_dft(n_pos):
    c = np.arange(HEAD_DIM)
    ang = 2.0 * np.pi * ((c[:, None] * c[None, :]) % HEAD_DIM) / HEAD_DIM
    scale = 1.0 / math.sqrt(n_pos * HEAD_DIM)
    bd = np.zeros((GROUP_W, 2 * GROUP_W), np.float32)
    for g in range(N_HEADS):
        sl = slice(g * HEAD_DIM, (g + 1) * HEAD_DIM)
        bd[sl, sl] = np.cos(ang) * scale
        bd[sl, GROUP_W + g * HEAD_DIM:GROUP_W + (g + 1) * HEAD_DIM] = -np.sin(ang) * scale
    return jnp.asarray(bd, bf16)


def _position_dft(n_pos):
    hi = max(n_pos // GRID_W, 1)
    lo = n_pos // hi
    col = jnp.arange(n_pos, dtype=jnp.int32)[None, :]
    a = 2.0 * np.pi * ((col * jnp.arange(hi, dtype=jnp.int32)[:, None] * lo) % n_pos).astype(f32) / n_pos
    b = 2.0 * np.pi * ((col * jnp.arange(lo, dtype=jnp.int32)[:, None]) % n_pos).astype(f32) / n_pos
    ca, sa, cb, sb = jnp.cos(a)[:, None, :], jnp.sin(a)[:, None, :], jnp.cos(b)[None, :, :], jnp.sin(b)[None, :, :]
    cn = (ca * cb - sa * sb).reshape(n_pos, n_pos)
    sn = (sa * cb + ca * sb).reshape(n_pos, n_pos)
    return cn.astype(bf16), sn.astype(bf16)


def _rope_tables(n):
    t = jnp.arange(n)
    rows = (t // GRID_W).astype(f32)
    cols = (t % GRID_W).astype(f32)
    n_freq = DIFF_QK_DIM // 4
    inv = ROPE_BASE ** (-jnp.arange(n_freq, dtype=f32) / n_freq)
    ang = jnp.concatenate([rows[:, None] * inv, cols[:, None] * inv], axis=-1)
    lane = np.arange(GROUP_W) % DIFF_QK_DIM
    src = (lane // 16) * n_freq + lane % n_freq
    sign = np.where((lane % 16) < n_freq, -1.0, 1.0).astype(np.float32)
    return jnp.cos(ang)[:, src], jnp.sin(ang)[:, src] * sign


def kernel(x, c, ctx, c_ctx, ada_w, ada_b, norm1_g, norm2_g, w_in, w_out, na_rpb, sgu_ln_g, sgu_ln_b, sgu_w, sgu_b,
           diff_lq1, diff_lk1, diff_lq2, diff_lk2, diff_subln_g, w_ff1, w_ff2, final_g):
    bsz, n, _ = x.shape
    n_ctx = ctx.shape[1]
    depth = w_in.shape[0]
    rows = n // GRID_W
    assert bsz <= COND_ROWS // 2

    cond = jnp.zeros((COND_ROWS, D_MODEL), f32).at[:bsz].set(c).at[COND_ROWS // 2].set(c_ctx)
    mods = _ada_mods(cond, ada_w, ada_b).reshape(depth * COND_ROWS * 6, 1, D_MODEL)

    cos_l, sin_l = _rope_tables(n)
    rope_lat = (cos_l * DIFF_Q_SCALE, sin_l * DIFF_Q_SCALE, cos_l, sin_l)
    ones = jnp.ones((n_ctx, GROUP_W), f32)
    rope_ctx = (ones * DIFF_Q_SCALE, ones * 0.0, ones, ones * 0.0)
    bd_lat, bd_ctx = _channel_dft(n), _channel_dft(n_ctx)
    dft_lat, dft_ctx = _position_dft(n), _position_dft(n_ctx)

    w_in_b, w_out_b = w_in.astype(bf16), w_out.astype(bf16)
    w1_b, w2_b, sgu_w_b = w_ff1.astype(bf16), w_ff2.astype(bf16), sgu_w.astype(bf16)
    fg = final_g.reshape(1, D_MODEL)
    g1 = norm1_g.reshape(depth, 1, D_MODEL)
    g2 = norm2_g.reshape(depth, 1, D_MODEL)
    lng = sgu_ln_g.reshape(depth, 1, GROUP_W)
    lnb = sgu_ln_b.reshape(depth, 1, GROUP_W)
    sgu_bias = jnp.repeat(sgu_b.transpose(0, 2, 1), HEAD_DIM, axis=2)
    subln = jnp.broadcast_to(diff_subln_g[:, :, None], (depth, HEAD_DIM, min(n, ATTN_Q_BLOCK)))
    lam_init = [0.8 - 0.6 * math.exp(-0.3 * l) for l in range(depth)]
    lam = (jnp.exp(jnp.sum(diff_lq1.astype(f32) * diff_lk1.astype(f32), axis=-1))
           - jnp.exp(jnp.sum(diff_lq2.astype(f32) * diff_lk2.astype(f32), axis=-1)) + jnp.asarray(lam_init, f32))
    na_bias = _na_bias_t(na_rpb, rows)
    band = NA_BAND_ROWS * GRID_W
    cx = ctx

    for l in range(depth):
        last = l == depth - 1
        post_scale = 1.0 - lam_init[l]

        p_lat = _proj_in(x, mods, l, False, g1, w_in_b, bd_lat, rope_lat, lng, lnb)
        p_ctx = _proj_in(cx, mods, l, True, g1, w_in_b, bd_ctx, rope_ctx, lng, lnb)
        acs, naq, nak, nav, sgu_u, sgu_v, dfq, dfk, dfv = p_lat
        c_acs, c_naq, c_nak, c_nav, c_sgu_u, c_sgu_v, c_dfq, c_dfk, c_dfv = p_ctx

        y_a = _fourier(*dft_lat, acs)
        ctx_na = (c_nak, c_nav, None, n_ctx, False)
        ctx_df = (c_dfk, c_dfv, None, n_ctx, False)
        y_b = _attn_t("na", l, naq, [(nak, nav, na_bias, band, True), ctx_na], lam, subln, 1, 1.0)
        y_c = _sgu(l, sgu_u, sgu_v, sgu_w_b, sgu_bias)
        y_d = _attn_t("diff", l, dfq, [(dfk, dfv, None, n, False), ctx_df], lam, subln, 2, post_scale)

        if not last:
            yc_a = _fourier(*dft_ctx, c_acs)
            yc_b = _attn_t("dense_ctx", l, c_naq, [ctx_na], lam, subln, 1, 1.0)
            yc_c = _sgu(l, c_sgu_u, c_sgu_v, sgu_w_b, sgu_bias)
            yc_d = _attn_t("diff_ctx", l, c_dfq, [ctx_df], lam, subln, 2, post_scale)
            cx = _post(cx, (yc_a, yc_b, yc_c, yc_d), mods, l, True, g2, w_out_b, w1_b, w2_b, fg, False)

        x = _post(x, (y_a, y_b, y_c, y_d), mods, l, False, g2, w_out_b, w1_b, w2_b, fg, last)
    return x
```
